```python
import math
import jax, jax.numpy as jnp
from jax import lax
import numpy as np

D_MODEL = 1024
BATCH = 2
SEQ = 16384
DEPTH = 1
DEC_BATCH = 32
DEC_SEQ = 2048
PAST_LEN = 128

ATT_HEADS = 8
ATT_HEAD_DIM = 64
ATT_WIDTH = ATT_HEADS * ATT_HEAD_DIM
DILATED_PATTERNS = ((128, 1), (512, 4), (2048, 16))
ATT_BLOCK = 64
NEG_INF = -1e30
DN_HEADS = 4
DN_HEAD_DIM = 128
DN_WIDTH = DN_HEADS * DN_HEAD_DIM
DN_CONV = 5
DN_CHUNK = 64
MIX_WIDTH = ATT_WIDTH + DN_WIDTH
IN_COLS = 3 * ATT_WIDTH + 4 * DN_WIDTH + 4 * DN_HEADS
FFN_DIM = 2816
FFN_CONV = 3
ROPE_THETA = 10000.0
NORM_EPS = 1e-6

kernel_name = 'hymba_dilated_swa_bi_gated_deltanet_convglu'


def _rms_norm(x, gain):
    xf = x.astype(jnp.float32)
    y = xf * lax.rsqrt(jnp.mean(xf * xf, axis=-1, keepdims=True) + NORM_EPS)
    return (y * gain.astype(jnp.float32)).astype(x.dtype)


def _l2_norm(x):
    return x * lax.rsqrt(jnp.sum(x * x, axis=-1, keepdims=True) + NORM_EPS)


def _dwconv_centered(x, w):
    k_width = w.shape[0]
    pad = k_width // 2
    length = x.shape[1]
    xp = jnp.pad(x, ((0, 0), (pad, pad), (0, 0)))
    out = xp[:, 0:length] * w[0]
    for j in range(1, k_width):
        out = out + xp[:, j:j + length] * w[j]
    return out


def _rotary(x):
    length, dh = x.shape[1], x.shape[-1]
    half = dh // 2
    inv_freq = 1.0 / (ROPE_THETA ** (jnp.arange(half, dtype=jnp.float32) * 2.0 / dh))
    ang = jnp.arange(length, dtype=jnp.float32)[:, None] * inv_freq[None, :]
    cos = jnp.cos(ang)[None, :, None, :]
    sin = jnp.sin(ang)[None, :, None, :]
    x1, x2 = x[..., :half], x[..., half:]
    return jnp.concatenate([x1 * cos - x2 * sin, x2 * cos + x1 * sin], axis=-1)


def _band_attention(q, k, v, radius):
    n, length, h, dh = q.shape
    blk = ATT_BLOCK
    nn = -(-radius // blk)
    nb = -(-length // blk)
    lp = nb * blk
    qp = jnp.pad(q, ((0, 0), (0, lp - length), (0, 0), (0, 0))).reshape(n, nb, blk, h, dh)
    pad_kv = ((0, 0), (nn * blk, lp - length + nn * blk), (0, 0), (0, 0))
    kp = jnp.pad(k, pad_kv).reshape(n, nb + 2 * nn, blk, h, dh)
    vp = jnp.pad(v, pad_kv).reshape(n, nb + 2 * nn, blk, h, dh)
    kw = jnp.concatenate([kp[:, j:j + nb] for j in range(2 * nn + 1)], axis=2)
    vw = jnp.concatenate([vp[:, j:j + nb] for j in range(2 * nn + 1)], axis=2)
    qpos = jnp.arange(nb)[:, None] * blk + jnp.arange(blk)[None, :]
    kpos = jnp.arange(nb)[:, None] * blk - nn * blk + jnp.arange((2 * nn + 1) * blk)[None, :]
    valid = ((jnp.abs(qpos[:, :, None] - kpos[:, None, :]) <= radius)
             & (kpos[:, None, :] >= 0) & (kpos[:, None, :] < length))
    s = jnp.einsum('nbqhd,nbkhd->nbhqk', qp, kw) * (1.0 / math.sqrt(dh))
    s = jnp.where(valid[None, :, None, :, :], s, NEG_INF)
    m = jnp.max(s, axis=-1)
    p = jnp.exp(s - m[..., None])
    den = jnp.sum(p, axis=-1)
    num = jnp.einsum('nbhqk,nbkhd->nbqhd', p, vw).reshape(n, lp, h, dh)[:, :length]
    m = m.transpose(0, 1, 3, 2).reshape(n, lp, h)[:, :length]
    den = den.transpose(0, 1, 3, 2).reshape(n, lp, h)[:, :length]
    return num, m, den


def _dilated_window(q, k, v, window, dilation):
    b, length, h, dh = q.shape
    ld = length // dilation

    def to_sub(t):
        return t.reshape(b, ld, dilation, h, dh).transpose(0, 2, 1, 3, 4).reshape(b * dilation, ld, h, dh)

    num, m, den = _band_attention(to_sub(q), to_sub(k), to_sub(v), window // (2 * dilation))
    num = num.reshape(b, dilation, ld, h, dh).transpose(0, 2, 1, 3, 4).reshape(b, length, h, dh)
    m = m.reshape(b, dilation, ld, h).transpose(0, 2, 1, 3).reshape(b, length, h)
    den = den.reshape(b, dilation, ld, h).transpose(0, 2, 1, 3).reshape(b, length, h)
    return num, m, den


def _dilated_attention(q, k, v):
    parts = [_dilated_window(q, k, v, w, d) for (w, d) in DILATED_PATTERNS]
    m_all = parts[0][1]
    for part in parts[1:]:
        m_all = jnp.maximum(m_all, part[1])
    scale0 = jnp.exp(parts[0][1] - m_all)
    num = parts[0][0] * scale0[..., None]
    den = parts[0][2] * scale0
    for part in parts[1:]:
        sc = jnp.exp(part[1] - m_all)
        num = num + part[0] * sc[..., None]
        den = den + part[2] * sc
    return num / den[..., None]


def _gated_delta_rule(q, k, v, g, beta):
    b, length, h, dk = q.shape
    dv = v.shape[-1]
    c = DN_CHUNK
    nc = length // c

    def chunks(t):
        return t.reshape(b, nc, c, h, -1).transpose(0, 1, 3, 2, 4)

    q, k, v = chunks(q), chunks(k), chunks(v)
    g = g.reshape(b, nc, c, h).transpose(0, 1, 3, 2)
    beta = beta.reshape(b, nc, c, h).transpose(0, 1, 3, 2)
    gc = jnp.cumsum(g, axis=-1)
    tri = jnp.tril(jnp.ones((c, c), dtype=bool))
    strict = jnp.tril(jnp.ones((c, c), dtype=bool), -1)
    decay = jnp.exp(jnp.where(tri, gc[..., :, None] - gc[..., None, :], -jnp.inf))
    kb = k * beta[..., None]
    lmat = jnp.where(strict, jnp.einsum('bnhid,bnhjd->bnhij', kb, k) * decay, 0.0)
    amat = lmat + jnp.eye(c, dtype=lmat.dtype)
    rhs = jnp.concatenate([v * beta[..., None], kb * jnp.exp(gc)[..., None]], axis=-1)
    sol = lax.linalg.triangular_solve(amat, rhs, left_side=True, lower=True, unit_diagonal=True)
    u, w = sol[..., :dv], sol[..., dv:]
    qk = jnp.einsum('bnhid,bnhjd->bnhij', q, k) * decay
    q_dec = q * jnp.exp(gc)[..., None]
    g_last = gc[..., -1]
    k_dec = k * jnp.exp(g_last[..., None] - gc)[..., None]
    xs = tuple(t.swapaxes(0, 1) for t in (u, w, qk, q_dec, k_dec, g_last))

    def step(state, inp):
        u_c, w_c, qk_c, qd_c, kd_c, gl_c = inp
        v_new = u_c - jnp.einsum('bhcd,bhde->bhce', w_c, state)
        o_c = jnp.einsum('bhcd,bhde->bhce', qd_c, state) + jnp.einsum('bhij,bhje->bhie', qk_c, v_new)
        state = state * jnp.exp(gl_c)[..., None, None] + jnp.einsum('bhcd,bhce->bhde', kd_c, v_new)
        return state, o_c

    s0 = jnp.zeros((b, h, dk, dv), jnp.float32)
    _, o = lax.scan(step, s0, xs)
    return o.transpose(1, 0, 3, 2, 4).reshape(b, length, h, dv)


def _layer(x, norm1, w_in, att_q_norm, att_k_norm, dn_conv_w, dn_a_log, dn_dt_bias, dn_out_norm,
           w_out, norm2, w_up, ffn_conv_w, ffn_conv_b, w_down):
    b, length, _ = x.shape
    f32 = jnp.float32
    n = _rms_norm(x, norm1)
    proj = n @ w_in
    o1 = ATT_WIDTH
    o3 = 3 * ATT_WIDTH
    o4 = o3 + 3 * DN_WIDTH
    o5 = o4 + DN_WIDTH
    o6 = o5 + 2 * DN_HEADS
    att_q, att_k, att_v, dn_qkv, dn_g, dn_b, dn_a = jnp.split(proj, [o1, 2 * o1, o3, o4, o5, o6], axis=-1)

    hs = (b, length, ATT_HEADS, ATT_HEAD_DIM)
    aq = _rotary(_rms_norm(att_q.reshape(hs), att_q_norm).astype(f32))
    ak = _rotary(_rms_norm(att_k.reshape(hs), att_k_norm).astype(f32))
    av = att_v.reshape(hs).astype(f32)
    att_out = _dilated_attention(aq, ak, av).reshape(b, length, ATT_WIDTH).astype(x.dtype)

    qkv = jax.nn.silu(_dwconv_centered(dn_qkv, dn_conv_w).astype(f32))
    dq, dk, dv = jnp.split(qkv, [DN_WIDTH, 2 * DN_WIDTH], axis=-1)
    ds = (b, length, DN_HEADS, DN_HEAD_DIM)
    dq = _l2_norm(dq.reshape(ds)) * (DN_HEAD_DIM ** -0.5)
    dk = _l2_norm(dk.reshape(ds))
    dv = dv.reshape(ds)
    beta = jax.nn.sigmoid(dn_b.astype(f32)).reshape(b, length, 2, DN_HEADS)
    log_decay = -jnp.exp(dn_a_log.astype(f32)) * jax.nn.softplus(
        dn_a.astype(f32).reshape(b, length, 2, DN_HEADS) + dn_dt_bias.astype(f32))
    o_fwd = _gated_delta_rule(dq, dk, dv, log_decay[:, :, 0], beta[:, :, 0])
    flip = lambda t: jnp.flip(t, axis=1)
    o_bwd = flip(_gated_delta_rule(flip(dq), flip(dk), flip(dv), flip(log_decay[:, :, 1]), flip(beta[:, :, 1])))
    o_dn = _rms_norm(o_fwd + o_bwd, dn_out_norm) * jax.nn.silu(dn_g.astype(f32)).reshape(ds)
    dn_out = o_dn.reshape(b, length, DN_WIDTH).astype(x.dtype)

    h = x + jnp.concatenate([att_out, dn_out], axis=-1) @ w_out

    u = _dwconv_centered(_rms_norm(h, norm2) @ w_up, ffn_conv_w) + ffn_conv_b
    gate, up = jnp.split(u, 2, axis=-1)
    return h + (jax.nn.silu(gate) * up) @ w_down


def setup_inputs(seed: int = 0) -> dict:
    key = jax.random.key(seed)
    ks = jax.random.split(key, 20)
    f32 = jnp.float32

    def nrm(k, shape, scale):
        return jax.random.normal(k, shape, f32) * scale

    x_prompt = nrm(ks[0], (BATCH, SEQ, D_MODEL), 1.0)
    x_sample = nrm(ks[1], (DEC_BATCH, DEC_SEQ, D_MODEL), 1.0)
    norm1 = 1.0 + nrm(ks[2], (DEPTH, D_MODEL), 0.02)
    w_in = nrm(ks[3], (DEPTH, D_MODEL, IN_COLS), D_MODEL ** -0.5)
    att_q_norm = 1.0 + nrm(ks[4], (DEPTH, ATT_HEAD_DIM), 0.02)
    att_k_norm = 1.0 + nrm(ks[5], (DEPTH, ATT_HEAD_DIM), 0.02)
    dn_conv_w = nrm(ks[6], (DEPTH, DN_CONV, 3 * DN_WIDTH), DN_CONV ** -0.5)
    dn_a_log = jnp.log(jax.random.uniform(ks[7], (DEPTH, 2, DN_HEADS), f32, 1.0, 16.0))
    dt = jnp.exp(jax.random.uniform(ks[8], (DEPTH, 2, DN_HEADS), f32, math.log(1e-3), math.log(1e-1)))
    dn_dt_bias = dt + jnp.log(-jnp.expm1(-dt))
    dn_out_norm = 1.0 + nrm(ks[9], (DEPTH, DN_HEAD_DIM), 0.02)
    w_out = nrm(ks[10], (DEPTH, MIX_WIDTH, D_MODEL), MIX_WIDTH ** -0.5)
    norm2 = 1.0 + nrm(ks[11], (DEPTH, D_MODEL), 0.02)
    w_up = nrm(ks[12], (DEPTH, D_MODEL, 2 * FFN_DIM), D_MODEL ** -0.5)
    ffn_conv_w = nrm(ks[13], (DEPTH, FFN_CONV, 2 * FFN_DIM), FFN_CONV ** -0.5)
    ffn_conv_b = nrm(ks[14], (DEPTH, 2 * FFN_DIM), 0.02)
    w_down = nrm(ks[15], (DEPTH, FFN_DIM, D_MODEL), FFN_DIM ** -0.5)
    return {'x_prompt': x_prompt, 'x_sample': x_sample, 'norm1': norm1, 'w_in': w_in,
            'att_q_norm': att_q_norm, 'att_k_norm': att_k_norm, 'dn_conv_w': dn_conv_w,
            'dn_a_log': dn_a_log, 'dn_dt_bias': dn_dt_bias, 'dn_out_norm': dn_out_norm,
            'w_out': w_out, 'norm2': norm2, 'w_up': w_up, 'ffn_conv_w': ffn_conv_w,
            'ffn_conv_b': ffn_conv_b, 'w_down': w_down}


def reference(x_prompt, x_sample, norm1, w_in, att_q_norm, att_k_norm, dn_conv_w, dn_a_log,
              dn_dt_bias, dn_out_norm, w_out, norm2, w_up, ffn_conv_w, ffn_conv_b, w_down):
    def trunk(x):
        for l in range(DEPTH):
            x = _layer(x, norm1[l], w_in[l], att_q_norm[l], att_k_norm[l], dn_conv_w[l], dn_a_log[l],
                       dn_dt_bias[l], dn_out_norm[l], w_out[l], norm2[l], w_up[l], ffn_conv_w[l],
                       ffn_conv_b[l], w_down[l])
        return x

    y_prompt = trunk(x_prompt)
    y_sample = trunk(x_sample)
    return (y_prompt, y_sample)
```

```python
import functools
import math

import jax
import jax.numpy as jnp
from jax import lax
from jax.experimental import pallas as pl
from jax.experimental.pallas import tpu as pltpu

D_MODEL = 1024
ATT_HEADS = 8
ATT_HEAD_DIM = 64
ATT_WIDTH = ATT_HEADS * ATT_HEAD_DIM
DN_HEADS = 4
DN_HEAD_DIM = 128
DN_WIDTH = DN_HEADS * DN_HEAD_DIM
DN_CONV = 5
DN_CHUNK = 64
FFN_DIM = 2816
FFN_CONV = 3
NORM_EPS = 1e-6

LANES = 128
HALO_ROWS = 16
N_SLABS = (ATT_WIDTH + DN_WIDTH) // LANES
FFN_CHUNK = 256
N_FFN_CHUNKS = FFN_DIM // FFN_CHUNK
VMEM_LIMIT = 56 * 1024 * 1024

F32 = jnp.float32
BF16 = jnp.bfloat16


def _rms_scale(xf):
    return lax.rsqrt(jnp.mean(xf * xf, axis=-1, keepdims=True) + NORM_EPS)


def _silu(x):
    return x * (1.0 / (1.0 + jnp.exp(-x)))


def _const_spec(shape):
    nd = len(shape)
    return pl.BlockSpec(shape, lambda b, i: (0,) * nd, pipeline_mode=pl.Buffered(1))


ATT_SLABS = ATT_WIDTH // LANES
DILATIONS = (1, 4, 16)
ROPE_HALF = ATT_HEAD_DIM // 2


def _inproj_kernel(x_ref, xp_ref, xn_ref, norm1_ref, wa_ref, wd_ref, wg_ref, wba_ref, qkg_ref,
                   cos_ref, sin_ref, convw_ref, gpar_ref,
                   q1_ref, q4_ref, q16_ref, k1_ref, k4_ref, k16_ref, v1_ref, v4_ref, v16_ref,
                   dq_ref, dk_ref, dv_ref, gate_ref, col_ref, row_ref,
                   conv_scr, perm_scr, *, tm):
    i = pl.program_id(1)
    last = pl.num_programs(1) - 1
    gain1 = norm1_ref[...]

    def normed(x):
        return (x * _rms_scale(x) * gain1).astype(BF16)

    n = normed(x_ref[0])
    nh = normed(jnp.concatenate([xp_ref[0, HALO_ROWS - 8:, :], xn_ref[0, :8, :]], axis=0))

    att = jnp.dot(n, wa_ref[...], preferred_element_type=F32)
    lane = lax.broadcasted_iota(jnp.int32, (1, LANES), 1)
    head_a = lane < ATT_HEAD_DIM
    first_half = (lane % ATT_HEAD_DIM) < ROPE_HALF
    cos = cos_ref[...]
    sin = sin_ref[...]
    for s in range(ATT_SLABS):
        for which in range(2):
            t = att[:, which * ATT_WIDTH + s * LANES: which * ATT_WIDTH + (s + 1) * LANES]
            ss = t * t
            sa = jnp.sum(jnp.where(head_a, ss, 0.0), axis=-1, keepdims=True)
            sb = jnp.sum(jnp.where(head_a, 0.0, ss), axis=-1, keepdims=True)
            ms = jnp.where(head_a, sa, sb) * (1.0 / ATT_HEAD_DIM)
            tn = t * lax.rsqrt(ms + NORM_EPS) * qkg_ref[which:which + 1, :]
            rot = jnp.where(first_half, pltpu.roll(tn, LANES - ROPE_HALF, 1), pltpu.roll(tn, ROPE_HALF, 1))
            r = tn * cos + rot * sin
            if which == 0:
                r = r * (1.0 / math.sqrt(ATT_HEAD_DIM))
            perm_scr[which * ATT_SLABS + s] = r
        perm_scr[2 * ATT_SLABS + s] = att[:, 2 * ATT_WIDTH + s * LANES: 2 * ATT_WIDTH + (s + 1) * LANES]
    outs = ((q1_ref, q4_ref, q16_ref), (k1_ref, k4_ref, k16_ref), (v1_ref, v4_ref, v16_ref))
    for which in range(3):
        o1, o4, o16 = outs[which]
        for s in range(ATT_SLABS):
            slab = which * ATT_SLABS + s
            o1[0, s] = perm_scr[slab].astype(BF16)
            for r in range(4):
                o4[0, r, s] = perm_scr[slab, pl.ds(r, tm // 4, stride=4), :].astype(BF16)
            for r in range(16):
                o16[0, r, s] = perm_scr[slab, pl.ds(r, tm // 16, stride=16), :].astype(BF16)

    row16 = lax.broadcasted_iota(jnp.int32, (16, 1), 0)
    halo_keep = jnp.where(row16 < 8, (i > 0).astype(F32), (i < last).astype(F32))
    dn = jnp.dot(n, wd_ref[...], preferred_element_type=F32)
    dnh = jnp.dot(nh, wd_ref[...], preferred_element_type=F32) * halo_keep
    conv_scr[0:8, :] = dnh[0:8]
    conv_scr[8:8 + tm, :] = dn
    conv_scr[8 + tm:16 + tm, :] = dnh[8:16]
    cw = convw_ref[...]
    y = dn * cw[2:3]
    for j in (0, 1, 3, 4):
        y = y + conv_scr[6 + j:6 + j + tm, :] * cw[j:j + 1]
    y = _silu(y)
    for h in range(DN_HEADS):
        qh = y[:, h * LANES:(h + 1) * LANES]
        kh = y[:, DN_WIDTH + h * LANES: DN_WIDTH + (h + 1) * LANES]
        vh = y[:, 2 * DN_WIDTH + h * LANES: 2 * DN_WIDTH + (h + 1) * LANES]
        qh = qh * lax.rsqrt(jnp.sum(qh * qh, axis=-1, keepdims=True) + NORM_EPS) * (DN_HEAD_DIM ** -0.5)
        kh = kh * lax.rsqrt(jnp.sum(kh * kh, axis=-1, keepdims=True) + NORM_EPS)
        dq_ref[0, h] = qh.astype(BF16)
        dk_ref[0, h] = kh.astype(BF16)
        dv_ref[0, h] = vh.astype(BF16)
    g = jnp.dot(n, wg_ref[...], preferred_element_type=F32)
    for h in range(DN_HEADS):
        gate_ref[0, h] = _silu(g[:, h * LANES:(h + 1) * LANES]).astype(BF16)

    ba = jnp.dot(n, wba_ref[...], preferred_element_type=F32)
    gp = gpar_ref[...]
    beta = 1.0 / (1.0 + jnp.exp(-ba))
    z = ba + gp[0:1]
    softplus = jnp.maximum(z, 0.0) + jnp.log(1.0 + jnp.exp(-jnp.abs(z)))
    colv = jnp.where(lane < 2 * DN_HEADS, beta, -jnp.exp(gp[1:2]) * softplus)
    t = colv.T
    gl = t[8:16]
    pos = lax.broadcasted_iota(jnp.int32, (1, tm), 1) % DN_CHUNK
    pre, suf = gl, gl
    step = 1
    while step < DN_CHUNK:
        pre = pre + jnp.where(pos >= step, pltpu.roll(pre, step, 1), 0.0)
        suf = suf + jnp.where(pos < DN_CHUNK - step, pltpu.roll(suf, tm - step, 1), 0.0)
        step *= 2
    sub = lax.broadcasted_iota(jnp.int32, (8, 1), 0)
    gc = jnp.where(sub < DN_HEADS, pre, suf)
    row_ref[0] = jnp.concatenate([t[0:8], gc], axis=0)
    col_ref[0] = jnp.concatenate([t[0:8], gc, t[16:]], axis=0).T


def _rope_tables(length):
    inv_freq = 1.0 / (10000.0 ** (jnp.arange(ROPE_HALF, dtype=F32) * 2.0 / ATT_HEAD_DIM))
    ang = jnp.arange(length, dtype=F32)[:, None] * inv_freq[None, :]
    cos, sin = jnp.cos(ang), jnp.sin(ang)
    cos128 = jnp.tile(cos, (1, LANES // ROPE_HALF))
    sin128 = jnp.tile(jnp.concatenate([-sin, sin], axis=-1), (1, LANES // ATT_HEAD_DIM))
    return cos128, sin128


def _inproj_call(x, norm1, w_in, att_q_norm, att_k_norm, dn_conv_w, dn_a_log, dn_dt_bias, *, tm=512):
    bsz, length, _ = x.shape
    nt = length // tm
    hb = tm // HALO_ROWS
    nhb = length // HALO_ROWS
    o3 = 3 * ATT_WIDTH
    o4 = o3 + 3 * DN_WIDTH
    o5 = o4 + DN_WIDTH
    wa = w_in[:, :o3].astype(BF16)
    wd = w_in[:, o3:o4].astype(BF16)
    wg = w_in[:, o4:o5].astype(BF16)
    wba = jnp.pad(w_in[:, o5:], ((0, 0), (0, LANES - 4 * DN_HEADS))).astype(BF16)
    qkg = jnp.stack([jnp.tile(att_q_norm, 2), jnp.tile(att_k_norm, 2)]
                    + [jnp.zeros((LANES,), F32)] * 6)
    convw = jnp.concatenate([dn_conv_w, jnp.zeros((3, 3 * DN_WIDTH), F32)], axis=0)
    pad_lanes = lambda v: jnp.pad(v.reshape(-1), (2 * DN_HEADS, LANES - 4 * DN_HEADS))
    gpar = jnp.stack([pad_lanes(dn_dt_bias), pad_lanes(dn_a_log)] + [jnp.zeros((LANES,), F32)] * 6)
    cos128, sin128 = _rope_tables(length)

    prev = lambda b, i: (b, jnp.maximum(i * hb - 1, 0), 0)
    nxt = lambda b, i: (b, jnp.minimum((i + 1) * hb, nhb - 1), 0)
    slab = jax.ShapeDtypeStruct((bsz, ATT_SLABS, length, LANES), BF16)
    slab_spec = pl.BlockSpec((1, ATT_SLABS, tm, LANES), lambda b, i: (b, 0, i, 0))
    out_shape, out_specs = [], []
    for _ in range(3):
        for d in DILATIONS:
            if d == 1:
                out_shape.append(slab)
                out_specs.append(slab_spec)
            else:
                out_shape.append(jax.ShapeDtypeStruct((bsz, d, ATT_SLABS, length // d, LANES), BF16))
                out_specs.append(pl.BlockSpec((1, d, ATT_SLABS, tm // d, LANES),
                                              lambda b, i: (b, 0, 0, i, 0)))
    out_shape += [slab] * 4
    out_specs += [slab_spec] * 4
    out_shape += [jax.ShapeDtypeStruct((bsz, length, LANES), F32),
                  jax.ShapeDtypeStruct((bsz, 16, length), F32)]
    out_specs += [pl.BlockSpec((1, tm, LANES), lambda b, i: (b, i, 0)),
                  pl.BlockSpec((1, 16, tm), lambda b, i: (b, 0, i))]

    return pl.pallas_call(
        functools.partial(_inproj_kernel, tm=tm),
        grid=(bsz, nt),
        in_specs=[
            pl.BlockSpec((1, tm, D_MODEL), lambda b, i: (b, i, 0)),
            pl.BlockSpec((1, HALO_ROWS, D_MODEL), prev),
            pl.BlockSpec((1, HALO_ROWS, D_MODEL), nxt),
            _const_spec((1, D_MODEL)),
            _const_spec((D_MODEL, o3)),
            _const_spec((D_MODEL, 3 * DN_WIDTH)),
            _const_spec((D_MODEL, DN_WIDTH)),
            _const_spec((D_MODEL, LANES)),
            _const_spec((8, LANES)),
            pl.BlockSpec((tm, LANES), lambda b, i: (i, 0)),
            pl.BlockSpec((tm, LANES), lambda b, i: (i, 0)),
            _const_spec((8, 3 * DN_WIDTH)),
            _const_spec((8, LANES)),
        ],
        out_specs=out_specs,
        out_shape=out_shape,
        scratch_shapes=[pltpu.VMEM((tm + 16, 3 * DN_WIDTH), F32),
                        pltpu.VMEM((3 * ATT_SLABS, tm, LANES), F32)],
        compiler_params=pltpu.CompilerParams(
            dimension_semantics=("parallel", "arbitrary"), vmem_limit_bytes=VMEM_LIMIT),
        name="inproj",
    )(x, x, x, norm1.reshape(1, D_MODEL), wa, wd, wg, wba, qkg, cos128, sin128, convw, gpar)


ATT_TILE = 1024
ATT_QB = 64
ATT_RADIUS = 64
ATT_KW = ATT_QB + 2 * ATT_RADIUS
NEG_INF = -1e30


def _attn_kernel(q1, k1, k1p, k1n, v1, v1p, v1n,
                 q4, k4, k4p, k4n, v4, v4p, v4n,
                 q16, k16, k16p, k16n, v16, v16p, v16n,
                 o_ref, acc_o, acc_m, bias_scr):
    i = pl.program_id(1)
    last = pl.num_programs(1) - 1
    lane = lax.broadcasted_iota(jnp.int32, (1, LANES), 1)
    is_a = lane < ATT_HEAD_DIM

    qi = lax.broadcasted_iota(jnp.int32, (ATT_QB, ATT_KW), 0)
    kj = lax.broadcasted_iota(jnp.int32, (ATT_QB, ATT_KW), 1)
    band = jnp.where((kj >= qi) & (kj <= qi + 2 * ATT_RADIUS), 0.0, NEG_INF)
    first_seq = (i == 0).astype(F32)
    last_seq = (i == last).astype(F32)
    before = jnp.where(kj < ATT_RADIUS, NEG_INF, 0.0)
    after = jnp.where(kj >= ATT_QB + ATT_RADIUS, NEG_INF, 0.0)
    bias_scr[0] = band
    bias_scr[1] = band + first_seq * before
    bias_scr[2] = band + last_seq * after
    bias_scr[3] = band + first_seq * before + last_seq * after

    def unit(stage, bias_idx, q_get, k_get, v_get, rows, out_rows=None):
        bias = bias_scr[bias_idx]
        for s in range(ATT_SLABS):
            qt = q_get(s)
            kw = k_get(s)
            vw = v_get(s)
            zero = jnp.zeros_like(qt)
            qq = jnp.concatenate([jnp.where(is_a, qt, zero), jnp.where(is_a, zero, qt)], axis=0)
            s2 = lax.dot_general(qq, kw, (((1,), (1,)), ((), ())), preferred_element_type=F32)
            one = jnp.ones_like(vw)
            vext = (jnp.where(is_a, vw, one), jnp.where(is_a, one, vw))
            res = []
            for hh in range(2):
                head = 2 * s + hh
                sc = s2[hh * ATT_QB:(hh + 1) * ATT_QB] + bias
                m_cur = jnp.max(sc, axis=-1, keepdims=True)
                if stage == 0:
                    m_new = jnp.broadcast_to(m_cur, (ATT_QB, LANES))
                else:
                    m_prev = acc_m[head, rows, :]
                    m_new = jnp.maximum(m_prev, m_cur)
                p = jnp.concatenate([jnp.exp(sc[:, :LANES] - m_new),
                                     jnp.exp(sc[:, LANES:] - m_new[:, :ATT_KW - LANES])], axis=-1)
                o = jnp.dot(p.astype(BF16), vext[hh], preferred_element_type=F32)
                if stage > 0:
                    o = o + jnp.exp(m_prev - m_new) * acc_o[head, rows, :]
                if stage < 2:
                    acc_m[head, rows, :] = m_new
                    acc_o[head, rows, :] = o
                else:
                    res.append(o * (1.0 / pltpu.roll(o, ATT_HEAD_DIM, 1)))
            if stage == 2:
                o_ref[0, s, out_rows, :] = jnp.where(is_a, res[0], res[1]).astype(BF16)

    def run_pattern(stage, d, q, k, kp, kn, v, vp, vn):
        ld = ATT_TILE // d
        nqb = ld // ATT_QB

        def at(ref, r, s, rows):
            return ref[0, s, rows, :] if d == 1 else ref[0, r, s, rows, :]

        def full(ref, r, s):
            return ref[0, s] if d == 1 else ref[0, r, s]

        def acc_rows(r, qb):
            start = qb * (ATT_QB * d) + r
            return pl.ds(start, ATT_QB) if d == 1 else pl.ds(start, ATT_QB, stride=d)

        def residue(r, carry):
            def q_at(qb):
                return lambda s: at(q, r, s, pl.ds(pl.multiple_of(qb * ATT_QB, ATT_QB), ATT_QB))

            if nqb == 1:
                unit(stage, 3, q_at(0),
                     lambda s: jnp.concatenate([full(kp, r, s), full(k, r, s), full(kn, r, s)], axis=0),
                     lambda s: jnp.concatenate([full(vp, r, s), full(v, r, s), full(vn, r, s)], axis=0),
                     acc_rows(r, 0), pl.ds(0, ATT_QB))
                return carry
            head_rows = pl.ds(0, ATT_QB + ATT_RADIUS)
            unit(stage, 1, q_at(0),
                 lambda s: jnp.concatenate([full(kp, r, s), at(k, r, s, head_rows)], axis=0),
                 lambda s: jnp.concatenate([full(vp, r, s), at(v, r, s, head_rows)], axis=0),
                 acc_rows(r, 0), pl.ds(0, ATT_QB))

            def interior(qb, c):
                win = pl.ds(pl.multiple_of(qb * ATT_QB - ATT_RADIUS, ATT_RADIUS), ATT_KW)
                unit(stage, 0, q_at(qb), lambda s: at(k, r, s, win), lambda s: at(v, r, s, win),
                     acc_rows(r, qb), pl.ds(pl.multiple_of(qb * ATT_QB, ATT_QB), ATT_QB))
                return c

            lax.fori_loop(1, nqb - 1, interior, 0)
            tail_rows = pl.ds(ld - ATT_QB - ATT_RADIUS, ATT_QB + ATT_RADIUS)
            unit(stage, 2, q_at(nqb - 1),
                 lambda s: jnp.concatenate([at(k, r, s, tail_rows), full(kn, r, s)], axis=0),
                 lambda s: jnp.concatenate([at(v, r, s, tail_rows), full(vn, r, s)], axis=0),
                 acc_rows(r, nqb - 1), pl.ds(ld - ATT_QB, ATT_QB))
            return carry

        if d == 1:
            residue(0, 0)
        else:
            lax.fori_loop(0, d, residue, 0)

    run_pattern(0, 16, q16, k16, k16p, k16n, v16, v16p, v16n)
    run_pattern(1, 4, q4, k4, k4p, k4n, v4, v4p, v4n)
    run_pattern(2, 1, q1, k1, k1p, k1n, v1, v1p, v1n)


def _attn_call(qkv):
    q1, q4, q16, k1, k4, k16, v1, v4, v16 = qkv
    bsz, _, length, _ = q1.shape
    nt = length // ATT_TILE
    args, specs = [], []
    for d, q, k, v in ((1, q1, k1, v1), (4, q4, k4, v4), (16, q16, k16, v16)):
        ld = ATT_TILE // d
        hb = ld // ATT_RADIUS
        nhb = length // d // ATT_RADIUS
        if d == 1:
            main = pl.BlockSpec((1, ATT_SLABS, ld, LANES), lambda b, i: (b, 0, i, 0))
            prev = pl.BlockSpec((1, ATT_SLABS, ATT_RADIUS, LANES),
                                lambda b, i, hb=hb: (b, 0, jnp.maximum(i * hb - 1, 0), 0))
            nxt = pl.BlockSpec((1, ATT_SLABS, ATT_RADIUS, LANES),
                               lambda b, i, hb=hb, nhb=nhb: (b, 0, jnp.minimum((i + 1) * hb, nhb - 1), 0))
        else:
            main = pl.BlockSpec((1, d, ATT_SLABS, ld, LANES), lambda b, i: (b, 0, 0, i, 0))
            prev = pl.BlockSpec((1, d, ATT_SLABS, ATT_RADIUS, LANES),
                                lambda b, i, hb=hb: (b, 0, 0, jnp.maximum(i * hb - 1, 0), 0))
            nxt = pl.BlockSpec((1, d, ATT_SLABS, ATT_RADIUS, LANES),
                               lambda b, i, hb=hb, nhb=nhb: (b, 0, 0, jnp.minimum((i + 1) * hb, nhb - 1), 0))
        args += [q, k, k, k, v, v, v]
        specs += [main, main, prev, nxt, main, prev, nxt]
    return pl.pallas_call(
        _attn_kernel,
        grid=(bsz, nt),
        in_specs=specs,
        out_specs=pl.BlockSpec((1, ATT_SLABS, ATT_TILE, LANES), lambda b, i: (b, 0, i, 0)),
        out_shape=jax.ShapeDtypeStruct((bsz, ATT_SLABS, length, LANES), BF16),
        scratch_shapes=[pltpu.VMEM((ATT_HEADS, ATT_TILE, LANES), F32),
                        pltpu.VMEM((ATT_HEADS, ATT_TILE, LANES), F32),
                        pltpu.VMEM((4, ATT_QB, ATT_KW), F32)],
        compiler_params=pltpu.CompilerParams(
            dimension_semantics=("parallel", "arbitrary"), vmem_limit_bytes=VMEM_LIMIT),
        name="dilated_attention",
    )(*args)


DN_TILE = 256
HIGHEST = lax.Precision.HIGHEST


def _dot_nt(a, b, **kw):
    return lax.dot_general(a, b, (((1,), (1,)), ((), ())), preferred_element_type=F32, **kw)


def _deltanet_kernel(*refs, reverse, tile):
    if reverse:
        dq_ref, dk_ref, dv_ref, col_ref, row_ref, o_ref, state = refs
    else:
        dq_ref, dk_ref, dv_ref, col_ref, row_ref, gate_ref, obwd_ref, gain_ref, o_ref, state = refs
    c_sz = DN_CHUNK
    nchunks = tile // c_sz

    @pl.when(pl.program_id(1) == 0)
    def _():
        state[...] = jnp.zeros_like(state)

    ii = lax.broadcasted_iota(jnp.int32, (c_sz, c_sz), 0)
    jj = lax.broadcasted_iota(jnp.int32, (c_sz, c_sz), 1)
    incl = (jj >= ii) if reverse else (jj <= ii)
    strict = (jj > ii) if reverse else (jj < ii)
    eye = (ii == jj).astype(F32)
    edge = 0 if reverse else c_sz - 1

    order = range(nchunks - 1, -1, -1) if reverse else range(nchunks)
    for c in order:
        rows = slice(c * c_sz, (c + 1) * c_sz)
        colt = col_ref[0, rows, :]
        rowt = row_ref[0, :, rows]
        for h in range(DN_HEADS):
            idx = (DN_HEADS if reverse else 0) + h
            beta = colt[:, idx:idx + 1]
            gcc = colt[:, 2 * DN_HEADS + idx:2 * DN_HEADS + idx + 1]
            gcr = rowt[2 * DN_HEADS + idx:2 * DN_HEADS + idx + 1, :]
            gl = gcc[edge:edge + 1, :]
            kb16 = dk_ref[0, h, rows, :]
            kf = kb16.astype(F32)
            qf = dq_ref[0, h, rows, :].astype(F32)
            vf = dv_ref[0, h, rows, :].astype(F32)
            egc = jnp.exp(gcc)
            kbeta = kf * beta
            decay = jnp.exp(jnp.where(incl, gcc - gcr, NEG_INF))
            lmat = jnp.where(strict, _dot_nt(kbeta.astype(BF16), kb16) * decay, 0.0)
            mpow = -lmat
            tinv = eye + mpow
            for _ in range(5):
                mpow = jnp.dot(mpow, mpow, preferred_element_type=F32, precision=HIGHEST)
                tinv = tinv + jnp.dot(tinv, mpow, preferred_element_type=F32, precision=HIGHEST)
            rhs = jnp.concatenate([vf * beta, kbeta * egc], axis=-1)
            sol = jnp.dot(tinv, rhs, preferred_element_type=F32, precision=HIGHEST)
            u = sol[:, :DN_HEAD_DIM]
            w = sol[:, DN_HEAD_DIM:]
            qk = _dot_nt(qf.astype(BF16), kb16) * decay
            qd = (qf * egc).astype(BF16)
            kd = (kf * jnp.exp(gl - gcc)).astype(BF16)
            st = state[h]
            st16 = st.astype(BF16)
            v_new = u - jnp.dot(w.astype(BF16), st16, preferred_element_type=F32)
            v16 = v_new.astype(BF16)
            o = (jnp.dot(qd, st16, preferred_element_type=F32)
                 + jnp.dot(qk.astype(BF16), v16, preferred_element_type=F32))
            state[h] = st * jnp.exp(gl) + lax.dot_general(
                kd, v16, (((0,), (0,)), ((), ())), preferred_element_type=F32)
            if reverse:
                o_ref[0, h, rows, :] = o
            else:
                tot = o + obwd_ref[0, h, rows, :]
                y = tot * _rms_scale(tot) * gain_ref[...] * gate_ref[0, h, rows, :].astype(F32)
                o_ref[0, h, rows, :] = y.astype(BF16)


def _deltanet_call(dq, dk, dv, col, row, gate, dn_out_norm, *, tile=DN_TILE):
    bsz, _, length, _ = dq.shape
    nt = length // tile

    def run(reverse, extra_args, extra_specs, out_dtype):
        pos = (lambda i: nt - 1 - i) if reverse else (lambda i: i)
        slab = pl.BlockSpec((1, DN_HEADS, tile, LANES), lambda b, i: (b, 0, pos(i), 0))
        return pl.pallas_call(
            functools.partial(_deltanet_kernel, reverse=reverse, tile=tile),
            grid=(bsz, nt),
            in_specs=[slab, slab, slab,
                      pl.BlockSpec((1, tile, LANES), lambda b, i: (b, pos(i), 0)),
                      pl.BlockSpec((1, 16, tile), lambda b, i: (b, 0, pos(i)))] + extra_specs(slab),
            out_specs=slab,
            out_shape=jax.ShapeDtypeStruct((bsz, DN_HEADS, length, LANES), out_dtype),
            scratch_shapes=[pltpu.VMEM((DN_HEADS, DN_HEAD_DIM, DN_HEAD_DIM), F32)],
            compiler_params=pltpu.CompilerParams(
                dimension_semantics=("parallel", "arbitrary"), vmem_limit_bytes=VMEM_LIMIT),
            name="deltanet_bwd" if reverse else "deltanet_fwd",
        )(dq, dk, dv, col, row, *extra_args)

    o_bwd = run(True, (), lambda slab: [], F32)
    return run(False, (gate, o_bwd, dn_out_norm.reshape(1, DN_HEAD_DIM)),
               lambda slab: [slab, slab, pl.BlockSpec((1, DN_HEAD_DIM), lambda b, i: (0, 0))], BF16)


def _ffn_kernel(x_ref, xp_ref, xn_ref, att_ref, attp_ref, attn_ref, dn_ref, dnp_ref, dnn_ref,
                wout_ref, norm2_ref, wg_ref, wu_ref, cg_ref, cu_ref, wd_ref, o_ref, g_scr, u_scr, *, tm):
    i = pl.program_id(1)
    last = pl.num_programs(1) - 1
    gain = norm2_ref[...]

    def hidden(x, slabs):
        mix = jnp.concatenate(slabs, axis=-1)
        h = x + jnp.dot(mix, wout_ref[...], preferred_element_type=F32)
        n = (h * _rms_scale(h) * gain).astype(BF16)
        return h, n

    half = N_SLABS // 2
    h, n2 = hidden(x_ref[0], [att_ref[0, s] for s in range(half)] + [dn_ref[0, s] for s in range(half)])
    xh = jnp.concatenate([xp_ref[0, HALO_ROWS - 8:, :], xn_ref[0, :8, :]], axis=0)
    mh = ([jnp.concatenate([attp_ref[0, s], attn_ref[0, s]], axis=0) for s in range(half)]
          + [jnp.concatenate([dnp_ref[0, s], dnn_ref[0, s]], axis=0) for s in range(half)])
    mh = [jnp.concatenate([m[HALO_ROWS - 8:HALO_ROWS], m[HALO_ROWS:HALO_ROWS + 8]], axis=0) for m in mh]
    _, n2h = hidden(xh, mh)
    row = lax.broadcasted_iota(jnp.int32, (16, 1), 0)
    halo_keep = jnp.where(row < 8, (i > 0).astype(F32), (i < last).astype(F32))

    acc = jnp.zeros((tm, D_MODEL), F32)
    for c in range(N_FFN_CHUNKS):
        slot = c % 2

        def conv(w_ref, cw_ref, scr):
            y = jnp.dot(n2, w_ref[c], preferred_element_type=F32)
            yh = jnp.dot(n2h, w_ref[c], preferred_element_type=F32)
            yh = yh * halo_keep
            scr[slot, 0:8, :] = yh[0:8]
            scr[slot, 8:8 + tm, :] = y
            scr[slot, 8 + tm:16 + tm, :] = yh[8:16]
            cw = cw_ref[c]
            return (scr[slot, 7:7 + tm, :] * cw[0:1] + y * cw[1:2]
                    + scr[slot, 9:9 + tm, :] * cw[2:3] + cw[3:4])

        g = conv(wg_ref, cg_ref, g_scr)
        u = conv(wu_ref, cu_ref, u_scr)
        act = (_silu(g) * u).astype(BF16)
        acc = acc + jnp.dot(act, wd_ref[c], preferred_element_type=F32)
    o_ref[0] = h + acc


def _ffn_call(x, att, dn, w_out, norm2, w_up, ffn_conv_w, ffn_conv_b, w_down, *, tm=512):
    bsz, length, _ = x.shape
    nt = length // tm
    hb = tm // HALO_ROWS
    nhb = length // HALO_ROWS

    def chunked_cols(w):
        return w.reshape(w.shape[0], N_FFN_CHUNKS, FFN_CHUNK).transpose(1, 0, 2)

    wg = chunked_cols(w_up[:, :FFN_DIM]).astype(BF16)
    wu = chunked_cols(w_up[:, FFN_DIM:]).astype(BF16)
    wd = w_down.reshape(N_FFN_CHUNKS, FFN_CHUNK, D_MODEL).astype(BF16)

    def conv_table(lo):
        t = jnp.concatenate([ffn_conv_w[:, lo:lo + FFN_DIM], ffn_conv_b[None, lo:lo + FFN_DIM],
                             jnp.zeros((4, FFN_DIM), F32)], axis=0)
        return chunked_cols(t)

    cg, cu = conv_table(0), conv_table(FFN_DIM)

    prev = lambda b, i: (b, jnp.maximum(i * hb - 1, 0), 0)
    nxt = lambda b, i: (b, jnp.minimum((i + 1) * hb, nhb - 1), 0)
    prev4 = lambda b, i: (b, 0, jnp.maximum(i * hb - 1, 0), 0)
    nxt4 = lambda b, i: (b, 0, jnp.minimum((i + 1) * hb, nhb - 1), 0)

    return pl.pallas_call(
        functools.partial(_ffn_kernel, tm=tm),
        grid=(bsz, nt),
        in_specs=[
            pl.BlockSpec((1, tm, D_MODEL), lambda b, i: (b, i, 0)),
            pl.BlockSpec((1, HALO_ROWS, D_MODEL), prev),
            pl.BlockSpec((1, HALO_ROWS, D_MODEL), nxt),
            pl.BlockSpec((1, N_SLABS // 2, tm, LANES), lambda b, i: (b, 0, i, 0)),
            pl.BlockSpec((1, N_SLABS // 2, HALO_ROWS, LANES), prev4),
            pl.BlockSpec((1, N_SLABS // 2, HALO_ROWS, LANES), nxt4),
            pl.BlockSpec((1, N_SLABS // 2, tm, LANES), lambda b, i: (b, 0, i, 0)),
            pl.BlockSpec((1, N_SLABS // 2, HALO_ROWS, LANES), prev4),
            pl.BlockSpec((1, N_SLABS // 2, HALO_ROWS, LANES), nxt4),
            _const_spec((D_MODEL, D_MODEL)),
            _const_spec((1, D_MODEL)),
            _const_spec((N_FFN_CHUNKS, D_MODEL, FFN_CHUNK)),
            _const_spec((N_FFN_CHUNKS, D_MODEL, FFN_CHUNK)),
            _const_spec((N_FFN_CHUNKS, 8, FFN_CHUNK)),
            _const_spec((N_FFN_CHUNKS, 8, FFN_CHUNK)),
            _const_spec((N_FFN_CHUNKS, FFN_CHUNK, D_MODEL)),
        ],
        out_specs=pl.BlockSpec((1, tm, D_MODEL), lambda b, i: (b, i, 0)),
        out_shape=jax.ShapeDtypeStruct((bsz, length, D_MODEL), F32),
        scratch_shapes=[pltpu.VMEM((2, tm + 16, FFN_CHUNK), F32),
                        pltpu.VMEM((2, tm + 16, FFN_CHUNK), F32)],
        compiler_params=pltpu.CompilerParams(
            dimension_semantics=("parallel", "arbitrary"), vmem_limit_bytes=VMEM_LIMIT),
        name="outproj_convglu",
    )(x, x, x, att, att, att, dn, dn, dn, w_out.astype(BF16), norm2.reshape(1, D_MODEL),
      wg, wu, cg, cu, wd)


def _layer(x, norm1, w_in, att_q_norm, att_k_norm, dn_conv_w, dn_a_log, dn_dt_bias, dn_out_norm,
           w_out, norm2, w_up, ffn_conv_w, ffn_conv_b, w_down):
    outs = _inproj_call(x, norm1, w_in, att_q_norm, att_k_norm, dn_conv_w, dn_a_log, dn_dt_bias)
    att = _attn_call(outs[:9])
    dq, dk, dv, gate, col, row = outs[9:]
    dn = _deltanet_call(dq, dk, dv, col, row, gate, dn_out_norm)
    return _ffn_call(x, att, dn, w_out, norm2, w_up, ffn_conv_w, ffn_conv_b, w_down)


def kernel(x_prompt, x_sample, norm1, w_in, att_q_norm, att_k_norm, dn_conv_w, dn_a_log, dn_dt_bias,
           dn_out_norm, w_out, norm2, w_up, ffn_conv_w, ffn_conv_b, w_down):
    def trunk(x):
        for l in range(norm1.shape[0]):
            x = _layer(x, norm1[l], w_in[l], att_q_norm[l], att_k_norm[l], dn_conv_w[l], dn_a_log[l],
                       dn_dt_bias[l], dn_out_norm[l], w_out[l], norm2[l], w_up[l], ffn_conv_w[l],
                       ffn_conv_b[l], w_down[l])
        return x

    return trunk(x_prompt), trunk(x_sample)
```

```python
import functools
import math

import jax
import jax.numpy as jnp
from jax import lax
from jax.experimental import pallas as pl
from jax.experimental.pallas import tpu as pltpu

D_MODEL = 1024
ATT_HEADS = 8
ATT_HEAD_DIM = 64
ATT_WIDTH = ATT_HEADS * ATT_HEAD_DIM
DN_HEADS = 4
DN_HEAD_DIM = 128
DN_WIDTH = DN_HEADS * DN_HEAD_DIM
DN_CONV = 5
DN_CHUNK = 64
FFN_DIM = 2816
FFN_CONV = 3
NORM_EPS = 1e-6

LANES = 128
HALO_ROWS = 16
N_SLABS = (ATT_WIDTH + DN_WIDTH) // LANES
FFN_CHUNK = 256
N_FFN_CHUNKS = FFN_DIM // FFN_CHUNK
VMEM_LIMIT = 56 * 1024 * 1024

F32 = jnp.float32
BF16 = jnp.bfloat16


def _rms_scale(xf):
    return lax.rsqrt(jnp.mean(xf * xf, axis=-1, keepdims=True) + NORM_EPS)


def _silu(x):
    return x * (1.0 / (1.0 + jnp.exp(-x)))


def _const_spec(shape):
    nd = len(shape)
    return pl.BlockSpec(shape, lambda b, i: (0,) * nd, pipeline_mode=pl.Buffered(1))


ATT_SLABS = ATT_WIDTH // LANES
DILATIONS = (1, 4, 16)
ROPE_HALF = ATT_HEAD_DIM // 2


def _inproj_kernel(x_ref, xp_ref, xn_ref, norm1_ref, wa_ref, wd_ref, wg_ref, wba_ref, qkg_ref,
                   cos_ref, sin_ref, convw_ref, gpar_ref,
                   q1_ref, q4_ref, q16_ref, k1_ref, k4_ref, k16_ref, v1_ref, v4_ref, v16_ref,
                   dq_ref, dk_ref, dv_ref, gate_ref, col_ref, row_ref,
                   conv_scr, perm_scr, *, tm):
    i = pl.program_id(1)
    last = pl.num_programs(1) - 1
    gain1 = norm1_ref[...]

    def normed(x):
        return (x * _rms_scale(x) * gain1).astype(BF16)

    n = normed(x_ref[0])
    nh = normed(jnp.concatenate([xp_ref[0, HALO_ROWS - 8:, :], xn_ref[0, :8, :]], axis=0))

    att = jnp.dot(n, wa_ref[...], preferred_element_type=F32)
    lane = lax.broadcasted_iota(jnp.int32, (1, LANES), 1)
    head_a = lane < ATT_HEAD_DIM
    first_half = (lane % ATT_HEAD_DIM) < ROPE_HALF
    cos = cos_ref[...]
    sin = sin_ref[...]
    for s in range(ATT_SLABS):
        for which in range(2):
            t = att[:, which * ATT_WIDTH + s * LANES: which * ATT_WIDTH + (s + 1) * LANES]
            ss = t * t
            sa = jnp.sum(jnp.where(head_a, ss, 0.0), axis=-1, keepdims=True)
            sb = jnp.sum(jnp.where(head_a, 0.0, ss), axis=-1, keepdims=True)
            ms = jnp.where(head_a, sa, sb) * (1.0 / ATT_HEAD_DIM)
            tn = t * lax.rsqrt(ms + NORM_EPS) * qkg_ref[which:which + 1, :]
            rot = jnp.where(first_half, pltpu.roll(tn, LANES - ROPE_HALF, 1), pltpu.roll(tn, ROPE_HALF, 1))
            r = tn * cos + rot * sin
            if which == 0:
                r = r * (1.0 / math.sqrt(ATT_HEAD_DIM))
            perm_scr[which * ATT_SLABS + s] = r
        perm_scr[2 * ATT_SLABS + s] = att[:, 2 * ATT_WIDTH + s * LANES: 2 * ATT_WIDTH + (s + 1) * LANES]
    outs = ((q1_ref, q4_ref, q16_ref), (k1_ref, k4_ref, k16_ref), (v1_ref, v4_ref, v16_ref))
    for which in range(3):
        o1, o4, o16 = outs[which]
        for s in range(ATT_SLABS):
            slab = which * ATT_SLABS + s
            o1[0, s] = perm_scr[slab].astype(BF16)
            for r in range(4):
                o4[0, r, s] = perm_scr[slab, pl.ds(r, tm // 4, stride=4), :].astype(BF16)
            for r in range(16):
                o16[0, r, s] = perm_scr[slab, pl.ds(r, tm // 16, stride=16), :].astype(BF16)

    row16 = lax.broadcasted_iota(jnp.int32, (16, 1), 0)
    halo_keep = jnp.where(row16 < 8, (i > 0).astype(F32), (i < last).astype(F32))
    dn = jnp.dot(n, wd_ref[...], preferred_element_type=F32)
    dnh = jnp.dot(nh, wd_ref[...], preferred_element_type=F32) * halo_keep
    conv_scr[0:8, :] = dnh[0:8]
    conv_scr[8:8 + tm, :] = dn
    conv_scr[8 + tm:16 + tm, :] = dnh[8:16]
    cw = convw_ref[...]
    y = dn * cw[2:3]
    for j in (0, 1, 3, 4):
        y = y + conv_scr[6 + j:6 + j + tm, :] * cw[j:j + 1]
    y = _silu(y)
    for h in range(DN_HEADS):
        qh = y[:, h * LANES:(h + 1) * LANES]
        kh = y[:, DN_WIDTH + h * LANES: DN_WIDTH + (h + 1) * LANES]
        vh = y[:, 2 * DN_WIDTH + h * LANES: 2 * DN_WIDTH + (h + 1) * LANES]
        qh = qh * lax.rsqrt(jnp.sum(qh * qh, axis=-1, keepdims=True) + NORM_EPS) * (DN_HEAD_DIM ** -0.5)
        kh = kh * lax.rsqrt(jnp.sum(kh * kh, axis=-1, keepdims=True) + NORM_EPS)
        dq_ref[0, h] = qh.astype(BF16)
        dk_ref[0, h] = kh.astype(BF16)
        dv_ref[0, h] = vh.astype(BF16)
    g = jnp.dot(n, wg_ref[...], preferred_element_type=F32)
    for h in range(DN_HEADS):
        gate_ref[0, h] = _silu(g[:, h * LANES:(h + 1) * LANES]).astype(BF16)

    ba = jnp.dot(n, wba_ref[...], preferred_element_type=F32)
    gp = gpar_ref[...]
    beta = 1.0 / (1.0 + jnp.exp(-ba))
    z = ba + gp[0:1]
    softplus = jnp.maximum(z, 0.0) + jnp.log(1.0 + jnp.exp(-jnp.abs(z)))
    colv = jnp.where(lane < 2 * DN_HEADS, beta, -jnp.exp(gp[1:2]) * softplus)
    t = colv.T
    gl = t[8:16]
    pos = lax.broadcasted_iota(jnp.int32, (1, tm), 1) % DN_CHUNK
    pre, suf = gl, gl
    step = 1
    while step < DN_CHUNK:
        pre = pre + jnp.where(pos >= step, pltpu.roll(pre, step, 1), 0.0)
        suf = suf + jnp.where(pos < DN_CHUNK - step, pltpu.roll(suf, tm - step, 1), 0.0)
        step *= 2
    sub = lax.broadcasted_iota(jnp.int32, (8, 1), 0)
    gc = jnp.where(sub < DN_HEADS, pre, suf)
    row_ref[0] = jnp.concatenate([t[0:8], gc], axis=0)
    col_ref[0] = jnp.concatenate([t[0:8], gc, t[16:]], axis=0).T


def _rope_tables(length):
    inv_freq = 1.0 / (10000.0 ** (jnp.arange(ROPE_HALF, dtype=F32) * 2.0 / ATT_HEAD_DIM))
    ang = jnp.arange(length, dtype=F32)[:, None] * inv_freq[None, :]
    cos, sin = jnp.cos(ang), jnp.sin(ang)
    cos128 = jnp.tile(cos, (1, LANES // ROPE_HALF))
    sin128 = jnp.tile(jnp.concatenate([-sin, sin], axis=-1), (1, LANES // ATT_HEAD_DIM))
    return cos128, sin128


def _inproj_call(x, norm1, w_in, att_q_norm, att_k_norm, dn_conv_w, dn_a_log, dn_dt_bias, *, tm=512):
    bsz, length, _ = x.shape
    nt = length // tm
    hb = tm // HALO_ROWS
    nhb = length // HALO_ROWS
    o3 = 3 * ATT_WIDTH
    o4 = o3 + 3 * DN_WIDTH
    o5 = o4 + DN_WIDTH
    wa = w_in[:, :o3].astype(BF16)
    wd = w_in[:, o3:o4].astype(BF16)
    wg = w_in[:, o4:o5].astype(BF16)
    wba = jnp.pad(w_in[:, o5:], ((0, 0), (0, LANES - 4 * DN_HEADS))).astype(BF16)
    qkg = jnp.stack([jnp.tile(att_q_norm, 2), jnp.tile(att_k_norm, 2)]
                    + [jnp.zeros((LANES,), F32)] * 6)
    convw = jnp.concatenate([dn_conv_w, jnp.zeros((3, 3 * DN_WIDTH), F32)], axis=0)
    pad_lanes = lambda v: jnp.pad(v.reshape(-1), (2 * DN_HEADS, LANES - 4 * DN_HEADS))
    gpar = jnp.stack([pad_lanes(dn_dt_bias), pad_lanes(dn_a_log)] + [jnp.zeros((LANES,), F32)] * 6)
    cos128, sin128 = _rope_tables(length)

    prev = lambda b, i: (b, jnp.maximum(i * hb - 1, 0), 0)
    nxt = lambda b, i: (b, jnp.minimum((i + 1) * hb, nhb - 1), 0)
    slab = jax.ShapeDtypeStruct((bsz, ATT_SLABS, length, LANES), BF16)
    slab_spec = pl.BlockSpec((1, ATT_SLABS, tm, LANES), lambda b, i: (b, 0, i, 0))
    out_shape, out_specs = [], []
    for _ in range(3):
        for d in DILATIONS:
            if d == 1:
                out_shape.append(slab)
                out_specs.append(slab_spec)
            else:
                out_shape.append(jax.ShapeDtypeStruct((bsz, d, ATT_SLABS, length // d, LANES), BF16))
                out_specs.append(pl.BlockSpec((1, d, ATT_SLABS, tm // d, LANES),
                                              lambda b, i: (b, 0, 0, i, 0)))
    out_shape += [slab] * 4
    out_specs += [slab_spec] * 4
    out_shape += [jax.ShapeDtypeStruct((bsz, length, LANES), F32),
                  jax.ShapeDtypeStruct((bsz, 16, length), F32)]
    out_specs += [pl.BlockSpec((1, tm, LANES), lambda b, i: (b, i, 0)),
                  pl.BlockSpec((1, 16, tm), lambda b, i: (b, 0, i))]

    return pl.pallas_call(
        functools.partial(_inproj_kernel, tm=tm),
        grid=(bsz, nt),
        in_specs=[
            pl.BlockSpec((1, tm, D_MODEL), lambda b, i: (b, i, 0)),
            pl.BlockSpec((1, HALO_ROWS, D_MODEL), prev),
            pl.BlockSpec((1, HALO_ROWS, D_MODEL), nxt),
            _const_spec((1, D_MODEL)),
            _const_spec((D_MODEL, o3)),
            _const_spec((D_MODEL, 3 * DN_WIDTH)),
            _const_spec((D_MODEL, DN_WIDTH)),
            _const_spec((D_MODEL, LANES)),
            _const_spec((8, LANES)),
            pl.BlockSpec((tm, LANES), lambda b, i: (i, 0)),
            pl.BlockSpec((tm, LANES), lambda b, i: (i, 0)),
            _const_spec((8, 3 * DN_WIDTH)),
            _const_spec((8, LANES)),
        ],
        out_specs=out_specs,
        out_shape=out_shape,
        scratch_shapes=[pltpu.VMEM((tm + 16, 3 * DN_WIDTH), F32),
                        pltpu.VMEM((3 * ATT_SLABS, tm, LANES), F32)],
        compiler_params=pltpu.CompilerParams(
            dimension_semantics=("parallel", "arbitrary"), vmem_limit_bytes=VMEM_LIMIT),
        name="inproj",
    )(x, x, x, norm1.reshape(1, D_MODEL), wa, wd, wg, wba, qkg, cos128, sin128, convw, gpar)


ATT_TILE = 1024
ATT_QB = 64
ATT_RADIUS = 64
ATT_KW = ATT_QB + 2 * ATT_RADIUS
NEG_INF = -1e30


def _attn_kernel(q1, k1, k1p, k1n, v1, v1p, v1n,
                 q4, k4, k4p, k4n, v4, v4p, v4n,
                 q16, k16, k16p, k16n, v16, v16p, v16n,
                 o_ref, acc_o, acc_m, bias_scr):
    i = pl.program_id(1)
    last = pl.num_programs(1) - 1
    lane = lax.broadcasted_iota(jnp.int32, (1, LANES), 1)
    is_a = lane < ATT_HEAD_DIM

    qi = lax.broadcasted_iota(jnp.int32, (ATT_QB, ATT_KW), 0)
    kj = lax.broadcasted_iota(jnp.int32, (ATT_QB, ATT_KW), 1)
    band = jnp.where((kj >= qi) & (kj <= qi + 2 * ATT_RADIUS), 0.0, NEG_INF)
    first_seq = (i == 0).astype(F32)
    last_seq = (i == last).astype(F32)
    before = jnp.where(kj < ATT_RADIUS, NEG_INF, 0.0)
    after = jnp.where(kj >= ATT_QB + ATT_RADIUS, NEG_INF, 0.0)
    bias_scr[0] = band
    bias_scr[1] = band + first_seq * before
    bias_scr[2] = band + last_seq * after
    bias_scr[3] = band + first_seq * before + last_seq * after

    def unit(stage, bias_idx, q_get, k_get, v_get, rows, out_rows=None):
        bias = bias_scr[bias_idx]
        for s in range(ATT_SLABS):
            qt = q_get(s)
            kw = k_get(s)
            vw = v_get(s)
            zero = jnp.zeros_like(qt)
            qq = jnp.concatenate([jnp.where(is_a, qt, zero), jnp.where(is_a, zero, qt)], axis=0)
            s2 = lax.dot_general(qq, kw, (((1,), (1,)), ((), ())), preferred_element_type=F32)
            one = jnp.ones_like(vw)
            vext = (jnp.where(is_a, vw, one), jnp.where(is_a, one, vw))
            res = []
            for hh in range(2):
                head = 2 * s + hh
                sc = s2[hh * ATT_QB:(hh + 1) * ATT_QB] + bias
                m_cur = jnp.max(sc, axis=-1, keepdims=True)
                if stage == 0:
                    m_new = jnp.broadcast_to(m_cur, (ATT_QB, LANES))
                else:
                    m_prev = acc_m[head, rows, :]
                    m_new = jnp.maximum(m_prev, m_cur)
                p = jnp.concatenate([jnp.exp(sc[:, :LANES] - m_new),
                                     jnp.exp(sc[:, LANES:] - m_new[:, :ATT_KW - LANES])], axis=-1)
                o = jnp.dot(p.astype(BF16), vext[hh], preferred_element_type=F32)
                if stage > 0:
                    o = o + jnp.exp(m_prev - m_new) * acc_o[head, rows, :]
                if stage < 2:
                    acc_m[head, rows, :] = m_new
                    acc_o[head, rows, :] = o
                else:
                    res.append(o * (1.0 / pltpu.roll(o, ATT_HEAD_DIM, 1)))
            if stage == 2:
                o_ref[0, s, out_rows, :] = jnp.where(is_a, res[0], res[1]).astype(BF16)

    def run_pattern(stage, d, q, k, kp, kn, v, vp, vn):
        ld = ATT_TILE // d
        nqb = ld // ATT_QB

        def at(ref, r, s, rows):
            return ref[0, s, rows, :] if d == 1 else ref[0, r, s, rows, :]

        def full(ref, r, s):
            return ref[0, s] if d == 1 else ref[0, r, s]

        def acc_rows(r, qb):
            start = qb * (ATT_QB * d) + r
            return pl.ds(start, ATT_QB) if d == 1 else pl.ds(start, ATT_QB, stride=d)

        def residue(r, carry):
            def q_at(qb):
                return lambda s: at(q, r, s, pl.ds(pl.multiple_of(qb * ATT_QB, ATT_QB), ATT_QB))

            if nqb == 1:
                unit(stage, 3, q_at(0),
                     lambda s: jnp.concatenate([full(kp, r, s), full(k, r, s), full(kn, r, s)], axis=0),
                     lambda s: jnp.concatenate([full(vp, r, s), full(v, r, s), full(vn, r, s)], axis=0),
                     acc_rows(r, 0), pl.ds(0, ATT_QB))
                return carry
            head_rows = pl.ds(0, ATT_QB + ATT_RADIUS)
            unit(stage, 1, q_at(0),
                 lambda s: jnp.concatenate([full(kp, r, s), at(k, r, s, head_rows)], axis=0),
                 lambda s: jnp.concatenate([full(vp, r, s), at(v, r, s, head_rows)], axis=0),
                 acc_rows(r, 0), pl.ds(0, ATT_QB))

            def interior(qb, c):
                win = pl.ds(pl.multiple_of(qb * ATT_QB - ATT_RADIUS, ATT_RADIUS), ATT_KW)
                unit(stage, 0, q_at(qb), lambda s: at(k, r, s, win), lambda s: at(v, r, s, win),
                     acc_rows(r, qb), pl.ds(pl.multiple_of(qb * ATT_QB, ATT_QB), ATT_QB))
                return c

            lax.fori_loop(1, nqb - 1, interior, 0)
            tail_rows = pl.ds(ld - ATT_QB - ATT_RADIUS, ATT_QB + ATT_RADIUS)
            unit(stage, 2, q_at(nqb - 1),
                 lambda s: jnp.concatenate([at(k, r, s, tail_rows), full(kn, r, s)], axis=0),
                 lambda s: jnp.concatenate([at(v, r, s, tail_rows), full(vn, r, s)], axis=0),
                 acc_rows(r, nqb - 1), pl.ds(ld - ATT_QB, ATT_QB))
            return carry

        if d == 1:
            residue(0, 0)
        else:
            lax.fori_loop(0, d, residue, 0)

    run_pattern(0, 16, q16, k16, k16p, k16n, v16, v16p, v16n)
    run_pattern(1, 4, q4, k4, k4p, k4n, v4, v4p, v4n)
    run_pattern(2, 1, q1, k1, k1p, k1n, v1, v1p, v1n)


def _attn_call(qkv):
    q1, q4, q16, k1, k4, k16, v1, v4, v16 = qkv
    bsz, _, length, _ = q1.shape
    nt = length // ATT_TILE
    args, specs = [], []
    for d, q, k, v in ((1, q1, k1, v1), (4, q4, k4, v4), (16, q16, k16, v16)):
        ld = ATT_TILE // d
        hb = ld // ATT_RADIUS
        nhb = length // d // ATT_RADIUS
        if d == 1:
            main = pl.BlockSpec((1, ATT_SLABS, ld, LANES), lambda b, i: (b, 0, i, 0))
            prev = pl.BlockSpec((1, ATT_SLABS, ATT_RADIUS, LANES),
                                lambda b, i, hb=hb: (b, 0, jnp.maximum(i * hb - 1, 0), 0))
            nxt = pl.BlockSpec((1, ATT_SLABS, ATT_RADIUS, LANES),
                               lambda b, i, hb=hb, nhb=nhb: (b, 0, jnp.minimum((i + 1) * hb, nhb - 1), 0))
        else:
            main = pl.BlockSpec((1, d, ATT_SLABS, ld, LANES), lambda b, i: (b, 0, 0, i, 0))
            prev = pl.BlockSpec((1, d, ATT_SLABS, ATT_RADIUS, LANES),
                                lambda b, i, hb=hb: (b, 0, 0, jnp.maximum(i * hb - 1, 0), 0))
            nxt = pl.BlockSpec((1, d, ATT_SLABS, ATT_RADIUS, LANES),
                               lambda b, i, hb=hb, nhb=nhb: (b, 0, 0, jnp.minimum((i + 1) * hb, nhb - 1), 0))
        args += [q, k, k, k, v, v, v]
        specs += [main, main, prev, nxt, main, prev, nxt]
    return pl.pallas_call(
        _attn_kernel,
        grid=(bsz, nt),
        in_specs=specs,
        out_specs=pl.BlockSpec((1, ATT_SLABS, ATT_TILE, LANES), lambda b, i: (b, 0, i, 0)),
        out_shape=jax.ShapeDtypeStruct((bsz, ATT_SLABS, length, LANES), BF16),
        scratch_shapes=[pltpu.VMEM((ATT_HEADS, ATT_TILE, LANES), F32),
                        pltpu.VMEM((ATT_HEADS, ATT_TILE, LANES), F32),
                        pltpu.VMEM((4, ATT_QB, ATT_KW), F32)],
        compiler_params=pltpu.CompilerParams(
            dimension_semantics=("parallel", "arbitrary"), vmem_limit_bytes=VMEM_LIMIT),
        name="dilated_attention",
    )(*args)


DN_TILE = 256
DN_ROWS = 2
DN_GROUP = 2


def _dot_nt(a, b, **kw):
    return lax.dot_general(a, b, (((1,), (1,)), ((), ())), preferred_element_type=F32, **kw)


def _deltanet_kernel(*refs, reverse, tile, rows_per_step):
    if reverse:
        dq_ref, dk_ref, dv_ref, col_ref, row_ref, o_ref, state = refs
    else:
        dq_ref, dk_ref, dv_ref, col_ref, row_ref, gate_ref, obwd_ref, gain_ref, o_ref, state = refs
    c_sz = DN_CHUNK
    nchunks = tile // c_sz

    @pl.when(pl.program_id(1) == 0)
    def _():
        state[...] = jnp.zeros_like(state)

    ii = lax.broadcasted_iota(jnp.int32, (c_sz, c_sz), 0)
    jj = lax.broadcasted_iota(jnp.int32, (c_sz, c_sz), 1)
    incl = (jj >= ii) if reverse else (jj <= ii)
    strict = (jj > ii) if reverse else (jj < ii)
    edge = 0 if reverse else c_sz - 1

    order = list(range(nchunks - 1, -1, -1)) if reverse else list(range(nchunks))
    rows_of = lambda c: slice(c * c_sz, (c + 1) * c_sz)
    bf = lambda t: t.astype(BF16)
    dot = functools.partial(jnp.dot, preferred_element_type=F32)

    pre = {}
    for g0 in range(0, nchunks, DN_GROUP):
        units = [(c, b, h) for c in order[g0:g0 + DN_GROUP] for b in range(rows_per_step)
                 for h in range(DN_HEADS)]
        colt = {(c, b): col_ref[b, rows_of(c), :] for c, b, _ in units}
        rowt = {(c, b): row_ref[b, :, rows_of(c)] for c, b, _ in units}
        v = {}
        for un in units:
            c, b, h = un
            idx = (DN_HEADS if reverse else 0) + h
            ct = colt[c, b]
            beta = ct[:, idx:idx + 1]
            gcc = ct[:, 2 * DN_HEADS + idx:2 * DN_HEADS + idx + 1]
            gcr = rowt[c, b][2 * DN_HEADS + idx:2 * DN_HEADS + idx + 1, :]
            gl = gcc[edge:edge + 1, :]
            k16 = dk_ref[b, h, rows_of(c), :]
            q16 = dq_ref[b, h, rows_of(c), :]
            kf = k16.astype(F32)
            kbeta = kf * beta
            egc = jnp.exp(gcc)
            v[un] = dict(
                beta=beta, egc=egc, kbeta=kbeta, eg=jnp.exp(gl),
                decay=jnp.exp(jnp.where(incl, gcc - gcr, NEG_INF)),
                kk=_dot_nt(bf(kbeta), k16), qk=_dot_nt(q16, k16),
                qd=bf(q16.astype(F32) * egc), kd=bf(kf * jnp.exp(gl - gcc)))
        mpow = {un: -jnp.where(strict, v[un]["kk"] * v[un]["decay"], 0.0) for un in units}
        tlow = dict(mpow)
        for _ in range(5):
            m16 = {un: bf(mpow[un]) for un in units}
            mpow = {un: dot(m16[un], m16[un]) for un in units}
            tlow = {un: tlow[un] + mpow[un] + dot(bf(tlow[un]), bf(mpow[un])) for un in units}
        for un in units:
            c, b, h = un
            vf = dv_ref[b, h, rows_of(c), :].astype(F32)
            rhs = jnp.concatenate([vf * v[un]["beta"], v[un]["kbeta"] * v[un]["egc"]], axis=-1)
            sol = rhs + dot(bf(tlow[un]), bf(rhs))
            pre[un] = dict(u=sol[:, :DN_HEAD_DIM],
                           wq=jnp.concatenate([bf(sol[:, DN_HEAD_DIM:]), v[un]["qd"]], axis=0),
                           qk=bf(v[un]["qk"] * v[un]["decay"]), kd=v[un]["kd"], eg=v[un]["eg"])

    chains = [(b, h) for b in range(rows_per_step) for h in range(DN_HEADS)]
    for c in order:
        st = {n: state[n[0] * DN_HEADS + n[1]] for n in chains}
        st16 = {n: bf(st[n]) for n in chains}
        ws = {n: dot(pre[(c,) + n]["wq"], st16[n]) for n in chains}
        v16 = {n: bf(pre[(c,) + n]["u"] - ws[n][:c_sz]) for n in chains}
        ov = {n: dot(pre[(c,) + n]["qk"], v16[n]) for n in chains}
        sv = {n: lax.dot_general(pre[(c,) + n]["kd"], v16[n], (((0,), (0,)), ((), ())),
                                 preferred_element_type=F32) for n in chains}
        for n in chains:
            b, h = n
            state[b * DN_HEADS + h] = st[n] * pre[(c,) + n]["eg"] + sv[n]
            o = ws[n][c_sz:] + ov[n]
            if reverse:
                o_ref[b, h, rows_of(c), :] = o
            else:
                tot = o + obwd_ref[b, h, rows_of(c), :]
                y = tot * _rms_scale(tot) * gain_ref[...] * gate_ref[b, h, rows_of(c), :].astype(F32)
                o_ref[b, h, rows_of(c), :] = y.astype(BF16)


def _deltanet_call(dq, dk, dv, col, row, gate, dn_out_norm, *, tile=DN_TILE, rows_per_step=DN_ROWS):
    bsz, _, length, _ = dq.shape
    nt = length // tile
    nb = bsz // rows_per_step

    def run(reverse, extra_args, extra_specs, out_dtype):
        pos = (lambda i: nt - 1 - i) if reverse else (lambda i: i)
        slab = pl.BlockSpec((rows_per_step, DN_HEADS, tile, LANES), lambda b, i: (b, 0, pos(i), 0))
        return pl.pallas_call(
            functools.partial(_deltanet_kernel, reverse=reverse, tile=tile, rows_per_step=rows_per_step),
            grid=(nb, nt),
            in_specs=[slab, slab, slab,
                      pl.BlockSpec((rows_per_step, tile, LANES), lambda b, i: (b, pos(i), 0)),
                      pl.BlockSpec((rows_per_step, 16, tile), lambda b, i: (b, 0, pos(i)))] + extra_specs(slab),
            out_specs=slab,
            out_shape=jax.ShapeDtypeStruct((bsz, DN_HEADS, length, LANES), out_dtype),
            scratch_shapes=[pltpu.VMEM((rows_per_step * DN_HEADS, DN_HEAD_DIM, DN_HEAD_DIM), F32)],
            compiler_params=pltpu.CompilerParams(
                dimension_semantics=("parallel", "arbitrary"), vmem_limit_bytes=VMEM_LIMIT),
            name="deltanet_bwd" if reverse else "deltanet_fwd",
        )(dq, dk, dv, col, row, *extra_args)

    o_bwd = run(True, (), lambda slab: [], F32)
    return run(False, (gate, o_bwd, dn_out_norm.reshape(1, DN_HEAD_DIM)),
               lambda slab: [slab, slab, pl.BlockSpec((1, DN_HEAD_DIM), lambda b, i: (0, 0))], BF16)


def _ffn_kernel(x_ref, xp_ref, xn_ref, att_ref, attp_ref, attn_ref, dn_ref, dnp_ref, dnn_ref,
                wout_ref, norm2_ref, wg_ref, wu_ref, cg_ref, cu_ref, wd_ref, o_ref, g_scr, u_scr, *, tm):
    i = pl.program_id(1)
    last = pl.num_programs(1) - 1
    gain = norm2_ref[...]

    def hidden(x, slabs):
        mix = jnp.concatenate(slabs, axis=-1)
        h = x + jnp.dot(mix, wout_ref[...], preferred_element_type=F32)
        n = (h * _rms_scale(h) * gain).astype(BF16)
        return h, n

    half = N_SLABS // 2
    h, n2 = hidden(x_ref[0], [att_ref[0, s] for s in range(half)] + [dn_ref[0, s] for s in range(half)])
    xh = jnp.concatenate([xp_ref[0, HALO_ROWS - 8:, :], xn_ref[0, :8, :]], axis=0)
    mh = ([jnp.concatenate([attp_ref[0, s], attn_ref[0, s]], axis=0) for s in range(half)]
          + [jnp.concatenate([dnp_ref[0, s], dnn_ref[0, s]], axis=0) for s in range(half)])
    mh = [jnp.concatenate([m[HALO_ROWS - 8:HALO_ROWS], m[HALO_ROWS:HALO_ROWS + 8]], axis=0) for m in mh]
    _, n2h = hidden(xh, mh)
    row = lax.broadcasted_iota(jnp.int32, (16, 1), 0)
    halo_keep = jnp.where(row < 8, (i > 0).astype(F32), (i < last).astype(F32))

    acc = jnp.zeros((tm, D_MODEL), F32)
    for c in range(N_FFN_CHUNKS):
        slot = c % 2

        def conv(w_ref, cw_ref, scr):
            y = jnp.dot(n2, w_ref[c], preferred_element_type=F32)
            yh = jnp.dot(n2h, w_ref[c], preferred_element_type=F32)
            yh = yh * halo_keep
            scr[slot, 0:8, :] = yh[0:8]
            scr[slot, 8:8 + tm, :] = y
            scr[slot, 8 + tm:16 + tm, :] = yh[8:16]
            cw = cw_ref[c]
            return (scr[slot, 7:7 + tm, :] * cw[0:1] + y * cw[1:2]
                    + scr[slot, 9:9 + tm, :] * cw[2:3] + cw[3:4])

        g = conv(wg_ref, cg_ref, g_scr)
        u = conv(wu_ref, cu_ref, u_scr)
        act = (_silu(g) * u).astype(BF16)
        acc = acc + jnp.dot(act, wd_ref[c], preferred_element_type=F32)
    o_ref[0] = h + acc


def _ffn_call(x, att, dn, w_out, norm2, w_up, ffn_conv_w, ffn_conv_b, w_down, *, tm=512):
    bsz, length, _ = x.shape
    nt = length // tm
    hb = tm // HALO_ROWS
    nhb = length // HALO_ROWS

    def chunked_cols(w):
        return w.reshape(w.shape[0], N_FFN_CHUNKS, FFN_CHUNK).transpose(1, 0, 2)

    wg = chunked_cols(w_up[:, :FFN_DIM]).astype(BF16)
    wu = chunked_cols(w_up[:, FFN_DIM:]).astype(BF16)
    wd = w_down.reshape(N_FFN_CHUNKS, FFN_CHUNK, D_MODEL).astype(BF16)

    def conv_table(lo):
        t = jnp.concatenate([ffn_conv_w[:, lo:lo + FFN_DIM], ffn_conv_b[None, lo:lo + FFN_DIM],
                             jnp.zeros((4, FFN_DIM), F32)], axis=0)
        return chunked_cols(t)

    cg, cu = conv_table(0), conv_table(FFN_DIM)

    prev = lambda b, i: (b, jnp.maximum(i * hb - 1, 0), 0)
    nxt = lambda b, i: (b, jnp.minimum((i + 1) * hb, nhb - 1), 0)
    prev4 = lambda b, i: (b, 0, jnp.maximum(i * hb - 1, 0), 0)
    nxt4 = lambda b, i: (b, 0, jnp.minimum((i + 1) * hb, nhb - 1), 0)

    return pl.pallas_call(
        functools.partial(_ffn_kernel, tm=tm),
        grid=(bsz, nt),
        in_specs=[
            pl.BlockSpec((1, tm, D_MODEL), lambda b, i: (b, i, 0)),
            pl.BlockSpec((1, HALO_ROWS, D_MODEL), prev),
            pl.BlockSpec((1, HALO_ROWS, D_MODEL), nxt),
            pl.BlockSpec((1, N_SLABS // 2, tm, LANES), lambda b, i: (b, 0, i, 0)),
            pl.BlockSpec((1, N_SLABS // 2, HALO_ROWS, LANES), prev4),
            pl.BlockSpec((1, N_SLABS // 2, HALO_ROWS, LANES), nxt4),
            pl.BlockSpec((1, N_SLABS // 2, tm, LANES), lambda b, i: (b, 0, i, 0)),
            pl.BlockSpec((1, N_SLABS // 2, HALO_ROWS, LANES), prev4),
            pl.BlockSpec((1, N_SLABS // 2, HALO_ROWS, LANES), nxt4),
            _const_spec((D_MODEL, D_MODEL)),
            _const_spec((1, D_MODEL)),
            _const_spec((N_FFN_CHUNKS, D_MODEL, FFN_CHUNK)),
            _const_spec((N_FFN_CHUNKS, D_MODEL, FFN_CHUNK)),
            _const_spec((N_FFN_CHUNKS, 8, FFN_CHUNK)),
            _const_spec((N_FFN_CHUNKS, 8, FFN_CHUNK)),
            _const_spec((N_FFN_CHUNKS, FFN_CHUNK, D_MODEL)),
        ],
        out_specs=pl.BlockSpec((1, tm, D_MODEL), lambda b, i: (b, i, 0)),
        out_shape=jax.ShapeDtypeStruct((bsz, length, D_MODEL), F32),
        scratch_shapes=[pltpu.VMEM((2, tm + 16, FFN_CHUNK), F32),
                        pltpu.VMEM((2, tm + 16, FFN_CHUNK), F32)],
        compiler_params=pltpu.CompilerParams(
            dimension_semantics=("parallel", "arbitrary"), vmem_limit_bytes=VMEM_LIMIT),
        name="outproj_convglu",
    )(x, x, x, att, att, att, dn, dn, dn, w_out.astype(BF16), norm2.reshape(1, D_MODEL),
      wg, wu, cg, cu, wd)


def _layer(x, norm1, w_in, att_q_norm, att_k_norm, dn_conv_w, dn_a_log, dn_dt_bias, dn_out_norm,
           w_out, norm2, w_up, ffn_conv_w, ffn_conv_b, w_down):
    outs = _inproj_call(x, norm1, w_in, att_q_norm, att_k_norm, dn_conv_w, dn_a_log, dn_dt_bias)
    att = _attn_call(outs[:9])
    dq, dk, dv, gate, col, row = outs[9:]
    dn = _deltanet_call(dq, dk, dv, col, row, gate, dn_out_norm)
    return _ffn_call(x, att, dn, w_out, norm2, w_up, ffn_conv_w, ffn_conv_b, w_down)


def kernel(x_prompt, x_sample, norm1, w_in, att_q_norm, att_k_norm, dn_conv_w, dn_a_log, dn_dt_bias,
           dn_out_norm, w_out, norm2, w_up, ffn_conv_w, ffn_conv_b, w_down):
    def trunk(x):
        for l in range(norm1.shape[0]):
            x = _layer(x, norm1[l], w_in[l], att_q_norm[l], att_k_norm[l], dn_conv_w[l], dn_a_log[l],
                       dn_dt_bias[l], dn_out_norm[l], w_out[l], norm2[l], w_up[l], ffn_conv_w[l],
                       ffn_conv_b[l], w_down[l])
        return x

    return trunk(x_prompt), trunk(x_sample)
```

```python
import functools
import math

import jax
import jax.numpy as jnp
from jax import lax
from jax.experimental import pallas as pl
from jax.experimental.pallas import tpu as pltpu

D_MODEL = 1024
ATT_HEADS = 8
ATT_HEAD_DIM = 64
ATT_WIDTH = ATT_HEADS * ATT_HEAD_DIM
DN_HEADS = 4
DN_HEAD_DIM = 128
DN_WIDTH = DN_HEADS * DN_HEAD_DIM
DN_CONV = 5
DN_CHUNK = 64
FFN_DIM = 2816
FFN_CONV = 3
NORM_EPS = 1e-6

LANES = 128
HALO_ROWS = 16
N_SLABS = (ATT_WIDTH + DN_WIDTH) // LANES
FFN_CHUNK = 256
N_FFN_CHUNKS = FFN_DIM // FFN_CHUNK
VMEM_LIMIT = 56 * 1024 * 1024

F32 = jnp.float32
BF16 = jnp.bfloat16


def _rms_scale(xf):
    return lax.rsqrt(jnp.mean(xf * xf, axis=-1, keepdims=True) + NORM_EPS)


def _silu(x):
    return x * (1.0 / (1.0 + jnp.exp(-x)))


def _const_spec(shape):
    nd = len(shape)
    return pl.BlockSpec(shape, lambda b, i: (0,) * nd, pipeline_mode=pl.Buffered(1))


ATT_SLABS = ATT_WIDTH // LANES
DILATIONS = (1, 4, 16)
ROPE_HALF = ATT_HEAD_DIM // 2


def _inproj_kernel(x_ref, xp_ref, xn_ref, norm1_ref, wa_ref, wd_ref, wg_ref, wba_ref, qkg_ref,
                   cos_ref, sin_ref, convw_ref, gpar_ref,
                   q1_ref, q4_ref, q16_ref, k1_ref, k4_ref, k16_ref, v1_ref, v4_ref, v16_ref,
                   dq_ref, dk_ref, dv_ref, gate_ref, col_ref, row_ref,
                   conv_scr, perm_scr, *, tm):
    i = pl.program_id(1)
    last = pl.num_programs(1) - 1
    gain1 = norm1_ref[...]

    def normed(x):
        return (x * _rms_scale(x) * gain1).astype(BF16)

    n = normed(x_ref[0])
    nh = normed(jnp.concatenate([xp_ref[0, HALO_ROWS - 8:, :], xn_ref[0, :8, :]], axis=0))

    row16 = lax.broadcasted_iota(jnp.int32, (16, 1), 0)
    halo_keep = jnp.where(row16 < 8, (i > 0).astype(F32), (i < last).astype(F32))
    att = jnp.dot(n, wa_ref[...], preferred_element_type=F32)
    dn_all = jnp.dot(jnp.concatenate([n, nh], axis=0), wd_ref[...], preferred_element_type=F32)
    dn = dn_all[:tm]
    dnh = dn_all[tm:] * halo_keep
    g = jnp.dot(n, wg_ref[...], preferred_element_type=F32)
    ba = jnp.dot(n, wba_ref[...], preferred_element_type=F32)

    lane = lax.broadcasted_iota(jnp.int32, (1, LANES), 1)
    head_a = lane < ATT_HEAD_DIM
    first_half = (lane % ATT_HEAD_DIM) < ROPE_HALF
    cos = cos_ref[...]
    sin = sin_ref[...]
    for s in range(ATT_SLABS):
        for which in range(2):
            t = att[:, which * ATT_WIDTH + s * LANES: which * ATT_WIDTH + (s + 1) * LANES]
            ss = t * t
            sa = jnp.sum(jnp.where(head_a, ss, 0.0), axis=-1, keepdims=True)
            sb = jnp.sum(jnp.where(head_a, 0.0, ss), axis=-1, keepdims=True)
            ms = jnp.where(head_a, sa, sb) * (1.0 / ATT_HEAD_DIM)
            tn = t * lax.rsqrt(ms + NORM_EPS) * qkg_ref[which:which + 1, :]
            rot = jnp.where(first_half, pltpu.roll(tn, LANES - ROPE_HALF, 1), pltpu.roll(tn, ROPE_HALF, 1))
            r = tn * cos + rot * sin
            if which == 0:
                r = r * (1.0 / math.sqrt(ATT_HEAD_DIM))
            perm_scr[which * ATT_SLABS + s] = r
        perm_scr[2 * ATT_SLABS + s] = att[:, 2 * ATT_WIDTH + s * LANES: 2 * ATT_WIDTH + (s + 1) * LANES]
    outs = ((q1_ref, q4_ref, q16_ref), (k1_ref, k4_ref, k16_ref), (v1_ref, v4_ref, v16_ref))
    for which in range(3):
        o1, o4, o16 = outs[which]
        for s in range(ATT_SLABS):
            slab = which * ATT_SLABS + s
            o1[0, s] = perm_scr[slab].astype(BF16)
            for r in range(4):
                o4[0, r, s] = perm_scr[slab, pl.ds(r, tm // 4, stride=4), :].astype(BF16)
            for r in range(16):
                o16[0, r, s] = perm_scr[slab, pl.ds(r, tm // 16, stride=16), :].astype(BF16)

    conv_scr[0:8, :] = dnh[0:8]
    conv_scr[8:8 + tm, :] = dn
    conv_scr[8 + tm:16 + tm, :] = dnh[8:16]
    cw = convw_ref[...]
    y = dn * cw[2:3]
    for j in (0, 1, 3, 4):
        y = y + conv_scr[6 + j:6 + j + tm, :] * cw[j:j + 1]
    y = _silu(y)
    for h in range(DN_HEADS):
        qh = y[:, h * LANES:(h + 1) * LANES]
        kh = y[:, DN_WIDTH + h * LANES: DN_WIDTH + (h + 1) * LANES]
        vh = y[:, 2 * DN_WIDTH + h * LANES: 2 * DN_WIDTH + (h + 1) * LANES]
        qh = qh * lax.rsqrt(jnp.sum(qh * qh, axis=-1, keepdims=True) + NORM_EPS) * (DN_HEAD_DIM ** -0.5)
        kh = kh * lax.rsqrt(jnp.sum(kh * kh, axis=-1, keepdims=True) + NORM_EPS)
        dq_ref[0, h] = qh.astype(BF16)
        dk_ref[0, h] = kh.astype(BF16)
        dv_ref[0, h] = vh.astype(BF16)
    for h in range(DN_HEADS):
        gate_ref[0, h] = _silu(g[:, h * LANES:(h + 1) * LANES]).astype(BF16)

    gp = gpar_ref[...]
    beta = 1.0 / (1.0 + jnp.exp(-ba))
    z = ba + gp[0:1]
    softplus = jnp.maximum(z, 0.0) + jnp.log(1.0 + jnp.exp(-jnp.abs(z)))
    colv = jnp.where(lane < 2 * DN_HEADS, beta, -jnp.exp(gp[1:2]) * softplus)
    t = colv.T
    gl = t[8:16]
    pos = lax.broadcasted_iota(jnp.int32, (1, tm), 1) % DN_CHUNK
    pre, suf = gl, gl
    step = 1
    while step < DN_CHUNK:
        pre = pre + jnp.where(pos >= step, pltpu.roll(pre, step, 1), 0.0)
        suf = suf + jnp.where(pos < DN_CHUNK - step, pltpu.roll(suf, tm - step, 1), 0.0)
        step *= 2
    sub = lax.broadcasted_iota(jnp.int32, (8, 1), 0)
    gc = jnp.where(sub < DN_HEADS, pre, suf)
    row_ref[0] = jnp.concatenate([t[0:8], gc], axis=0)
    col_ref[0] = jnp.concatenate([t[0:8], gc, t[16:]], axis=0).T


def _rope_tables(length):
    inv_freq = 1.0 / (10000.0 ** (jnp.arange(ROPE_HALF, dtype=F32) * 2.0 / ATT_HEAD_DIM))
    ang = jnp.arange(length, dtype=F32)[:, None] * inv_freq[None, :]
    cos, sin = jnp.cos(ang), jnp.sin(ang)
    cos128 = jnp.tile(cos, (1, LANES // ROPE_HALF))
    sin128 = jnp.tile(jnp.concatenate([-sin, sin], axis=-1), (1, LANES // ATT_HEAD_DIM))
    return cos128, sin128


def _inproj_call(x, norm1, w_in, att_q_norm, att_k_norm, dn_conv_w, dn_a_log, dn_dt_bias, *, tm=512):
    bsz, length, _ = x.shape
    nt = length // tm
    hb = tm // HALO_ROWS
    nhb = length // HALO_ROWS
    o3 = 3 * ATT_WIDTH
    o4 = o3 + 3 * DN_WIDTH
    o5 = o4 + DN_WIDTH
    wa = w_in[:, :o3].astype(BF16)
    wd = w_in[:, o3:o4].astype(BF16)
    wg = w_in[:, o4:o5].astype(BF16)
    wba = jnp.pad(w_in[:, o5:], ((0, 0), (0, LANES - 4 * DN_HEADS))).astype(BF16)
    qkg = jnp.stack([jnp.tile(att_q_norm, 2), jnp.tile(att_k_norm, 2)]
                    + [jnp.zeros((LANES,), F32)] * 6)
    convw = jnp.concatenate([dn_conv_w, jnp.zeros((3, 3 * DN_WIDTH), F32)], axis=0)
    pad_lanes = lambda v: jnp.pad(v.reshape(-1), (2 * DN_HEADS, LANES - 4 * DN_HEADS))
    gpar = jnp.stack([pad_lanes(dn_dt_bias), pad_lanes(dn_a_log)] + [jnp.zeros((LANES,), F32)] * 6)
    cos128, sin128 = _rope_tables(length)

    prev = lambda b, i: (b, jnp.maximum(i * hb - 1, 0), 0)
    nxt = lambda b, i: (b, jnp.minimum((i + 1) * hb, nhb - 1), 0)
    slab = jax.ShapeDtypeStruct((bsz, ATT_SLABS, length, LANES), BF16)
    slab_spec = pl.BlockSpec((1, ATT_SLABS, tm, LANES), lambda b, i: (b, 0, i, 0))
    out_shape, out_specs = [], []
    for _ in range(3):
        for d in DILATIONS:
            if d == 1:
                out_shape.append(slab)
                out_specs.append(slab_spec)
            else:
                out_shape.append(jax.ShapeDtypeStruct((bsz, d, ATT_SLABS, length // d, LANES), BF16))
                out_specs.append(pl.BlockSpec((1, d, ATT_SLABS, tm // d, LANES),
                                              lambda b, i: (b, 0, 0, i, 0)))
    out_shape += [slab] * 4
    out_specs += [slab_spec] * 4
    out_shape += [jax.ShapeDtypeStruct((bsz, length, LANES), F32),
                  jax.ShapeDtypeStruct((bsz, 16, length), F32)]
    out_specs += [pl.BlockSpec((1, tm, LANES), lambda b, i: (b, i, 0)),
                  pl.BlockSpec((1, 16, tm), lambda b, i: (b, 0, i))]

    return pl.pallas_call(
        functools.partial(_inproj_kernel, tm=tm),
        grid=(bsz, nt),
        in_specs=[
            pl.BlockSpec((1, tm, D_MODEL), lambda b, i: (b, i, 0)),
            pl.BlockSpec((1, HALO_ROWS, D_MODEL), prev),
            pl.BlockSpec((1, HALO_ROWS, D_MODEL), nxt),
            _const_spec((1, D_MODEL)),
            _const_spec((D_MODEL, o3)),
            _const_spec((D_MODEL, 3 * DN_WIDTH)),
            _const_spec((D_MODEL, DN_WIDTH)),
            _const_spec((D_MODEL, LANES)),
            _const_spec((8, LANES)),
            pl.BlockSpec((tm, LANES), lambda b, i: (i, 0)),
            pl.BlockSpec((tm, LANES), lambda b, i: (i, 0)),
            _const_spec((8, 3 * DN_WIDTH)),
            _const_spec((8, LANES)),
        ],
        out_specs=out_specs,
        out_shape=out_shape,
        scratch_shapes=[pltpu.VMEM((tm + 16, 3 * DN_WIDTH), F32),
                        pltpu.VMEM((3 * ATT_SLABS, tm, LANES), F32)],
        compiler_params=pltpu.CompilerParams(
            dimension_semantics=("parallel", "arbitrary"), vmem_limit_bytes=VMEM_LIMIT),
        name="inproj",
    )(x, x, x, norm1.reshape(1, D_MODEL), wa, wd, wg, wba, qkg, cos128, sin128, convw, gpar)


ATT_TILE = 1024
ATT_QB = 64
ATT_RADIUS = 64
ATT_KW = ATT_QB + 2 * ATT_RADIUS
ATT_GROUP = 2
NEG_INF = -1e30


def _attn_kernel(q1, k1, k1p, k1n, v1, v1p, v1n,
                 q4, k4, k4p, k4n, v4, v4p, v4n,
                 q16, k16, k16p, k16n, v16, v16p, v16n,
                 o_ref, acc_o, acc_m, bias_scr):
    i = pl.program_id(1)
    last = pl.num_programs(1) - 1
    lane = lax.broadcasted_iota(jnp.int32, (1, LANES), 1)
    is_a = lane < ATT_HEAD_DIM

    qi = lax.broadcasted_iota(jnp.int32, (ATT_QB, ATT_KW), 0)
    kj = lax.broadcasted_iota(jnp.int32, (ATT_QB, ATT_KW), 1)
    band = jnp.where((kj >= qi) & (kj <= qi + 2 * ATT_RADIUS), 0.0, NEG_INF)
    first_seq = (i == 0).astype(F32)
    last_seq = (i == last).astype(F32)
    before = jnp.where(kj < ATT_RADIUS, NEG_INF, 0.0)
    after = jnp.where(kj >= ATT_QB + ATT_RADIUS, NEG_INF, 0.0)
    bias_scr[0] = band
    bias_scr[1] = band + first_seq * before
    bias_scr[2] = band + last_seq * after
    bias_scr[3] = band + first_seq * before + last_seq * after

    def unit_group(stage, blocks):
        items = [(bi, s) for bi in range(len(blocks)) for s in range(ATT_SLABS)]
        heads = [(bi, s, hh) for bi, s in items for hh in range(2)]
        s2, vext = {}, {}
        for bi, s in items:
            _, q_get, k_get, v_get, _, _ = blocks[bi]
            qt = q_get(s)
            vw = v_get(s)
            zero = jnp.zeros_like(qt)
            qq = jnp.concatenate([jnp.where(is_a, qt, zero), jnp.where(is_a, zero, qt)], axis=0)
            s2[bi, s] = lax.dot_general(qq, k_get(s), (((1,), (1,)), ((), ())),
                                        preferred_element_type=F32)
            one = jnp.ones_like(vw)
            vext[bi, s] = (jnp.where(is_a, vw, one), jnp.where(is_a, one, vw))
        m_prev, m_new, p16 = {}, {}, {}
        for it in heads:
            bi, s, hh = it
            rows = blocks[bi][4]
            sc = s2[bi, s][hh * ATT_QB:(hh + 1) * ATT_QB] + bias_scr[blocks[bi][0]]
            m_cur = jnp.max(sc, axis=-1, keepdims=True)
            if stage == 0:
                m_new[it] = jnp.broadcast_to(m_cur, (ATT_QB, LANES))
            else:
                m_prev[it] = acc_m[2 * s + hh, rows, :]
                m_new[it] = jnp.maximum(m_prev[it], m_cur)
            p16[it] = jnp.concatenate(
                [jnp.exp(sc[:, :LANES] - m_new[it]),
                 jnp.exp(sc[:, LANES:] - m_new[it][:, :ATT_KW - LANES])], axis=-1).astype(BF16)
        o = {it: jnp.dot(p16[it], vext[it[0], it[1]][it[2]], preferred_element_type=F32)
             for it in heads}
        for it in heads:
            bi, s, hh = it
            rows = blocks[bi][4]
            if stage > 0:
                o[it] = o[it] + jnp.exp(m_prev[it] - m_new[it]) * acc_o[2 * s + hh, rows, :]
            if stage < 2:
                acc_m[2 * s + hh, rows, :] = m_new[it]
                acc_o[2 * s + hh, rows, :] = o[it]
        if stage == 2:
            for bi, s in items:
                norm = [o[bi, s, hh] * (1.0 / pltpu.roll(o[bi, s, hh], ATT_HEAD_DIM, 1)) for hh in range(2)]
                o_ref[0, s, blocks[bi][5], :] = jnp.where(is_a, norm[0], norm[1]).astype(BF16)

    def run_pattern(stage, d, q, k, kp, kn, v, vp, vn):
        ld = ATT_TILE // d
        nqb = ld // ATT_QB

        def at(ref, r, s, rows):
            return ref[0, s, rows, :] if d == 1 else ref[0, r, s, rows, :]

        def full(ref, r, s):
            return ref[0, s] if d == 1 else ref[0, r, s]

        def acc_rows(r, qb):
            start = qb * (ATT_QB * d) + r
            return pl.ds(start, ATT_QB) if d == 1 else pl.ds(start, ATT_QB, stride=d)

        def block(r, qb):
            q_rows = pl.ds(qb * ATT_QB, ATT_QB) if isinstance(qb, int) else \
                pl.ds(pl.multiple_of(qb * ATT_QB, ATT_QB), ATT_QB)
            q_get = lambda s: at(q, r, s, q_rows)
            if nqb == 1:
                pieces = lambda m, mp, mn: (lambda s: jnp.concatenate(
                    [full(mp, r, s), full(m, r, s), full(mn, r, s)], axis=0))
                return (3, q_get, pieces(k, kp, kn), pieces(v, vp, vn), acc_rows(r, 0), q_rows)
            if isinstance(qb, int) and qb == 0:
                head_rows = pl.ds(0, ATT_QB + ATT_RADIUS)
                pieces = lambda m, mp: (lambda s: jnp.concatenate(
                    [full(mp, r, s), at(m, r, s, head_rows)], axis=0))
                return (1, q_get, pieces(k, kp), pieces(v, vp), acc_rows(r, 0), q_rows)
            if isinstance(qb, int) and qb == nqb - 1:
                tail_rows = pl.ds(ld - ATT_QB - ATT_RADIUS, ATT_QB + ATT_RADIUS)
                pieces = lambda m, mn: (lambda s: jnp.concatenate(
                    [at(m, r, s, tail_rows), full(mn, r, s)], axis=0))
                return (2, q_get, pieces(k, kn), pieces(v, vn), acc_rows(r, qb), q_rows)
            start = qb * ATT_QB - ATT_RADIUS
            win = pl.ds(start if isinstance(qb, int) else pl.multiple_of(start, ATT_RADIUS), ATT_KW)
            return (0, q_get, lambda s: at(k, r, s, win), lambda s: at(v, r, s, win),
                    acc_rows(r, qb), q_rows)

        if nqb == 1:
            def pair(j, carry):
                unit_group(stage, [block(2 * j, 0), block(2 * j + 1, 0)])
                return carry
            lax.fori_loop(0, d // ATT_GROUP, pair, 0)
            return

        def residue(r, carry):
            unit_group(stage, [block(r, 0), block(r, 1)])

            def interior(j, c):
                unit_group(stage, [block(r, 2 * j), block(r, 2 * j + 1)])
                return c

            if nqb > 2 * ATT_GROUP:
                lax.fori_loop(1, nqb // ATT_GROUP - 1, interior, 0)
            unit_group(stage, [block(r, nqb - 2), block(r, nqb - 1)])
            return carry

        if d == 1:
            residue(0, 0)
        else:
            lax.fori_loop(0, d, residue, 0)

    run_pattern(0, 16, q16, k16, k16p, k16n, v16, v16p, v16n)
    run_pattern(1, 4, q4, k4, k4p, k4n, v4, v4p, v4n)
    run_pattern(2, 1, q1, k1, k1p, k1n, v1, v1p, v1n)


def _attn_call(qkv):
    q1, q4, q16, k1, k4, k16, v1, v4, v16 = qkv
    bsz, _, length, _ = q1.shape
    nt = length // ATT_TILE
    args, specs = [], []
    for d, q, k, v in ((1, q1, k1, v1), (4, q4, k4, v4), (16, q16, k16, v16)):
        ld = ATT_TILE // d
        hb = ld // ATT_RADIUS
        nhb = length // d // ATT_RADIUS
        if d == 1:
            main = pl.BlockSpec((1, ATT_SLABS, ld, LANES), lambda b, i: (b, 0, i, 0))
            prev = pl.BlockSpec((1, ATT_SLABS, ATT_RADIUS, LANES),
                                lambda b, i, hb=hb: (b, 0, jnp.maximum(i * hb - 1, 0), 0))
            nxt = pl.BlockSpec((1, ATT_SLABS, ATT_RADIUS, LANES),
                               lambda b, i, hb=hb, nhb=nhb: (b, 0, jnp.minimum((i + 1) * hb, nhb - 1), 0))
        else:
            main = pl.BlockSpec((1, d, ATT_SLABS, ld, LANES), lambda b, i: (b, 0, 0, i, 0))
            prev = pl.BlockSpec((1, d, ATT_SLABS, ATT_RADIUS, LANES),
                                lambda b, i, hb=hb: (b, 0, 0, jnp.maximum(i * hb - 1, 0), 0))
            nxt = pl.BlockSpec((1, d, ATT_SLABS, ATT_RADIUS, LANES),
                               lambda b, i, hb=hb, nhb=nhb: (b, 0, 0, jnp.minimum((i + 1) * hb, nhb - 1), 0))
        args += [q, k, k, k, v, v, v]
        specs += [main, main, prev, nxt, main, prev, nxt]
    return pl.pallas_call(
        _attn_kernel,
        grid=(bsz, nt),
        in_specs=specs,
        out_specs=pl.BlockSpec((1, ATT_SLABS, ATT_TILE, LANES), lambda b, i: (b, 0, i, 0)),
        out_shape=jax.ShapeDtypeStruct((bsz, ATT_SLABS, length, LANES), BF16),
        scratch_shapes=[pltpu.VMEM((ATT_HEADS, ATT_TILE, LANES), F32),
                        pltpu.VMEM((ATT_HEADS, ATT_TILE, LANES), F32),
                        pltpu.VMEM((4, ATT_QB, ATT_KW), F32)],
        compiler_params=pltpu.CompilerParams(
            dimension_semantics=("parallel", "arbitrary"), vmem_limit_bytes=VMEM_LIMIT),
        name="dilated_attention",
    )(*args)


DN_TILE = 256
DN_ROWS = 2
DN_GROUP = 2


def _dot_nt(a, b, **kw):
    return lax.dot_general(a, b, (((1,), (1,)), ((), ())), preferred_element_type=F32, **kw)


def _deltanet_kernel(*refs, reverse, tile, rows_per_step):
    if reverse:
        dq_ref, dk_ref, dv_ref, col_ref, row_ref, o_ref, state = refs
    else:
        dq_ref, dk_ref, dv_ref, col_ref, row_ref, gate_ref, obwd_ref, gain_ref, o_ref, state = refs
    c_sz = DN_CHUNK
    nchunks = tile // c_sz

    @pl.when(pl.program_id(1) == 0)
    def _():
        state[...] = jnp.zeros_like(state)

    ii = lax.broadcasted_iota(jnp.int32, (c_sz, c_sz), 0)
    jj = lax.broadcasted_iota(jnp.int32, (c_sz, c_sz), 1)
    incl = (jj >= ii) if reverse else (jj <= ii)
    strict = (jj > ii) if reverse else (jj < ii)
    edge = 0 if reverse else c_sz - 1

    order = list(range(nchunks - 1, -1, -1)) if reverse else list(range(nchunks))
    rows_of = lambda c: slice(c * c_sz, (c + 1) * c_sz)
    bf = lambda t: t.astype(BF16)
    dot = functools.partial(jnp.dot, preferred_element_type=F32)

    pre = {}
    for g0 in range(0, nchunks, DN_GROUP):
        units = [(c, b, h) for c in order[g0:g0 + DN_GROUP] for b in range(rows_per_step)
                 for h in range(DN_HEADS)]
        colt = {(c, b): col_ref[b, rows_of(c), :] for c, b, _ in units}
        rowt = {(c, b): row_ref[b, :, rows_of(c)] for c, b, _ in units}
        v = {}
        for un in units:
            c, b, h = un
            idx = (DN_HEADS if reverse else 0) + h
            ct = colt[c, b]
            beta = ct[:, idx:idx + 1]
            gcc = ct[:, 2 * DN_HEADS + idx:2 * DN_HEADS + idx + 1]
            gcr = rowt[c, b][2 * DN_HEADS + idx:2 * DN_HEADS + idx + 1, :]
            gl = gcc[edge:edge + 1, :]
            k16 = dk_ref[b, h, rows_of(c), :]
            q16 = dq_ref[b, h, rows_of(c), :]
            kf = k16.astype(F32)
            kbeta = kf * beta
            egc = jnp.exp(gcc)
            v[un] = dict(
                beta=beta, egc=egc, kbeta=kbeta, eg=jnp.exp(gl),
                decay=jnp.exp(jnp.where(incl, gcc - gcr, NEG_INF)),
                kk=_dot_nt(bf(kbeta), k16), qk=_dot_nt(q16, k16),
                qd=bf(q16.astype(F32) * egc), kd=bf(kf * jnp.exp(gl - gcc)))
        mpow = {un: -jnp.where(strict, v[un]["kk"] * v[un]["decay"], 0.0) for un in units}
        tlow = dict(mpow)
        for _ in range(5):
            m16 = {un: bf(mpow[un]) for un in units}
            mpow = {un: dot(m16[un], m16[un]) for un in units}
            tlow = {un: tlow[un] + mpow[un] + dot(bf(tlow[un]), bf(mpow[un])) for un in units}
        for un in units:
            c, b, h = un
            vf = dv_ref[b, h, rows_of(c), :].astype(F32)
            rhs = jnp.concatenate([vf * v[un]["beta"], v[un]["kbeta"] * v[un]["egc"]], axis=-1)
            sol = rhs + dot(bf(tlow[un]), bf(rhs))
            pre[un] = dict(u=sol[:, :DN_HEAD_DIM],
                           wq=jnp.concatenate([bf(sol[:, DN_HEAD_DIM:]), v[un]["qd"]], axis=0),
                           qk=bf(v[un]["qk"] * v[un]["decay"]), kd=v[un]["kd"], eg=v[un]["eg"])

    chains = [(b, h) for b in range(rows_per_step) for h in range(DN_HEADS)]
    for c in order:
        st = {n: state[n[0] * DN_HEADS + n[1]] for n in chains}
        st16 = {n: bf(st[n]) for n in chains}
        ws = {n: dot(pre[(c,) + n]["wq"], st16[n]) for n in chains}
        v16 = {n: bf(pre[(c,) + n]["u"] - ws[n][:c_sz]) for n in chains}
        ov = {n: dot(pre[(c,) + n]["qk"], v16[n]) for n in chains}
        sv = {n: lax.dot_general(pre[(c,) + n]["kd"], v16[n], (((0,), (0,)), ((), ())),
                                 preferred_element_type=F32) for n in chains}
        for n in chains:
            b, h = n
            state[b * DN_HEADS + h] = st[n] * pre[(c,) + n]["eg"] + sv[n]
            o = ws[n][c_sz:] + ov[n]
            if reverse:
                o_ref[b, h, rows_of(c), :] = o
            else:
                tot = o + obwd_ref[b, h, rows_of(c), :]
                y = tot * _rms_scale(tot) * gain_ref[...] * gate_ref[b, h, rows_of(c), :].astype(F32)
                o_ref[b, h, rows_of(c), :] = y.astype(BF16)


def _deltanet_call(dq, dk, dv, col, row, gate, dn_out_norm, *, tile=DN_TILE, rows_per_step=DN_ROWS):
    bsz, _, length, _ = dq.shape
    assert bsz % rows_per_step == 0 and length % tile == 0
    nt = length // tile
    nb = bsz // rows_per_step

    def run(reverse, extra_args, extra_specs, out_dtype):
        pos = (lambda i: nt - 1 - i) if reverse else (lambda i: i)
        slab = pl.BlockSpec((rows_per_step, DN_HEADS, tile, LANES), lambda b, i: (b, 0, pos(i), 0))
        return pl.pallas_call(
            functools.partial(_deltanet_kernel, reverse=reverse, tile=tile, rows_per_step=rows_per_step),
            grid=(nb, nt),
            in_specs=[slab, slab, slab,
                      pl.BlockSpec((rows_per_step, tile, LANES), lambda b, i: (b, pos(i), 0)),
                      pl.BlockSpec((rows_per_step, 16, tile), lambda b, i: (b, 0, pos(i)))] + extra_specs(slab),
            out_specs=slab,
            out_shape=jax.ShapeDtypeStruct((bsz, DN_HEADS, length, LANES), out_dtype),
            scratch_shapes=[pltpu.VMEM((rows_per_step * DN_HEADS, DN_HEAD_DIM, DN_HEAD_DIM), F32)],
            compiler_params=pltpu.CompilerParams(
                dimension_semantics=("parallel", "arbitrary"), vmem_limit_bytes=VMEM_LIMIT),
            name="deltanet_bwd" if reverse else "deltanet_fwd",
        )(dq, dk, dv, col, row, *extra_args)

    o_bwd = run(True, (), lambda slab: [], F32)
    return run(False, (gate, o_bwd, dn_out_norm.reshape(1, DN_HEAD_DIM)),
               lambda slab: [slab, slab, pl.BlockSpec((1, DN_HEAD_DIM), lambda b, i: (0, 0))], BF16)


def _ffn_kernel(x_ref, xp_ref, xn_ref, att_ref, attp_ref, attn_ref, dn_ref, dnp_ref, dnn_ref,
                wout_ref, norm2_ref, wg_ref, wu_ref, cg_ref, cu_ref, wd_ref, o_ref, g_scr, u_scr, *, tm):
    i = pl.program_id(1)
    last = pl.num_programs(1) - 1
    gain = norm2_ref[...]

    def hidden(x, slabs):
        mix = jnp.concatenate(slabs, axis=-1)
        h = x + jnp.dot(mix, wout_ref[...], preferred_element_type=F32)
        n = (h * _rms_scale(h) * gain).astype(BF16)
        return h, n

    half = N_SLABS // 2
    xh = jnp.concatenate([xp_ref[0, HALO_ROWS - 8:, :], xn_ref[0, :8, :]], axis=0)
    mh = ([jnp.concatenate([attp_ref[0, s], attn_ref[0, s]], axis=0) for s in range(half)]
          + [jnp.concatenate([dnp_ref[0, s], dnn_ref[0, s]], axis=0) for s in range(half)])
    mh = [jnp.concatenate([m[HALO_ROWS - 8:HALO_ROWS], m[HALO_ROWS:HALO_ROWS + 8]], axis=0) for m in mh]
    main = [att_ref[0, s] for s in range(half)] + [dn_ref[0, s] for s in range(half)]
    h_all, n2_all = hidden(jnp.concatenate([x_ref[0], xh], axis=0),
                           [jnp.concatenate([a, b], axis=0) for a, b in zip(main, mh)])
    h = h_all[:tm]
    row = lax.broadcasted_iota(jnp.int32, (16, 1), 0)
    halo_keep = jnp.where(row < 8, (i > 0).astype(F32), (i < last).astype(F32))

    def up_proj(c):
        ys = [jnp.dot(n2_all, w_ref[c], preferred_element_type=F32) for w_ref in (wg_ref, wu_ref)]
        return [(y[:tm], y[tm:] * halo_keep) for y in ys]

    def conv(c, y, yh, cw_ref, scr):
        slot = c % 2
        scr[slot, 0:8, :] = yh[0:8]
        scr[slot, 8:8 + tm, :] = y
        scr[slot, 8 + tm:16 + tm, :] = yh[8:16]
        cw = cw_ref[c]
        return (scr[slot, 7:7 + tm, :] * cw[0:1] + y * cw[1:2]
                + scr[slot, 9:9 + tm, :] * cw[2:3] + cw[3:4])

    acc = jnp.zeros((tm, D_MODEL), F32)
    ahead = up_proj(0)
    for c in range(N_FFN_CHUNKS):
        (yg, ygh), (yu, yuh) = ahead
        if c + 1 < N_FFN_CHUNKS:
            ahead = up_proj(c + 1)
        g = conv(c, yg, ygh, cg_ref, g_scr)
        u = conv(c, yu, yuh, cu_ref, u_scr)
        act = (_silu(g) * u).astype(BF16)
        acc = acc + jnp.dot(act, wd_ref[c], preferred_element_type=F32)
    o_ref[0] = h + acc


def _ffn_call(x, att, dn, w_out, norm2, w_up, ffn_conv_w, ffn_conv_b, w_down, *, tm=512):
    bsz, length, _ = x.shape
    nt = length // tm
    hb = tm // HALO_ROWS
    nhb = length // HALO_ROWS

    def chunked_cols(w):
        return w.reshape(w.shape[0], N_FFN_CHUNKS, FFN_CHUNK).transpose(1, 0, 2)

    wg = chunked_cols(w_up[:, :FFN_DIM]).astype(BF16)
    wu = chunked_cols(w_up[:, FFN_DIM:]).astype(BF16)
    wd = w_down.reshape(N_FFN_CHUNKS, FFN_CHUNK, D_MODEL).astype(BF16)

    def conv_table(lo):
        t = jnp.concatenate([ffn_conv_w[:, lo:lo + FFN_DIM], ffn_conv_b[None, lo:lo + FFN_DIM],
                             jnp.zeros((4, FFN_DIM), F32)], axis=0)
        return chunked_cols(t)

    cg, cu = conv_table(0), conv_table(FFN_DIM)

    prev = lambda b, i: (b, jnp.maximum(i * hb - 1, 0), 0)
    nxt = lambda b, i: (b, jnp.minimum((i + 1) * hb, nhb - 1), 0)
    prev4 = lambda b, i: (b, 0, jnp.maximum(i * hb - 1, 0), 0)
    nxt4 = lambda b, i: (b, 0, jnp.minimum((i + 1) * hb, nhb - 1), 0)

    return pl.pallas_call(
        functools.partial(_ffn_kernel, tm=tm),
        grid=(bsz, nt),
        in_specs=[
            pl.BlockSpec((1, tm, D_MODEL), lambda b, i: (b, i, 0)),
            pl.BlockSpec((1, HALO_ROWS, D_MODEL), prev),
            pl.BlockSpec((1, HALO_ROWS, D_MODEL), nxt),
            pl.BlockSpec((1, N_SLABS // 2, tm, LANES), lambda b, i: (b, 0, i, 0)),
            pl.BlockSpec((1, N_SLABS // 2, HALO_ROWS, LANES), prev4),
            pl.BlockSpec((1, N_SLABS // 2, HALO_ROWS, LANES), nxt4),
            pl.BlockSpec((1, N_SLABS // 2, tm, LANES), lambda b, i: (b, 0, i, 0)),
            pl.BlockSpec((1, N_SLABS // 2, HALO_ROWS, LANES), prev4),
            pl.BlockSpec((1, N_SLABS // 2, HALO_ROWS, LANES), nxt4),
            _const_spec((D_MODEL, D_MODEL)),
            _const_spec((1, D_MODEL)),
            _const_spec((N_FFN_CHUNKS, D_MODEL, FFN_CHUNK)),
            _const_spec((N_FFN_CHUNKS, D_MODEL, FFN_CHUNK)),
            _const_spec((N_FFN_CHUNKS, 8, FFN_CHUNK)),
            _const_spec((N_FFN_CHUNKS, 8, FFN_CHUNK)),
            _const_spec((N_FFN_CHUNKS, FFN_CHUNK, D_MODEL)),
        ],
        out_specs=pl.BlockSpec((1, tm, D_MODEL), lambda b, i: (b, i, 0)),
        out_shape=jax.ShapeDtypeStruct((bsz, length, D_MODEL), F32),
        scratch_shapes=[pltpu.VMEM((2, tm + 16, FFN_CHUNK), F32),
                        pltpu.VMEM((2, tm + 16, FFN_CHUNK), F32)],
        compiler_params=pltpu.CompilerParams(
            dimension_semantics=("parallel", "arbitrary"), vmem_limit_bytes=VMEM_LIMIT),
        name="outproj_convglu",
    )(x, x, x, att, att, att, dn, dn, dn, w_out.astype(BF16), norm2.reshape(1, D_MODEL),
      wg, wu, cg, cu, wd)


def _layer(x, norm1, w_in, att_q_norm, att_k_norm, dn_conv_w, dn_a_log, dn_dt_bias, dn_out_norm,
           w_out, norm2, w_up, ffn_conv_w, ffn_conv_b, w_down):
    outs = _inproj_call(x, norm1, w_in, att_q_norm, att_k_norm, dn_conv_w, dn_a_log, dn_dt_bias)
    att = _attn_call(outs[:9])
    dq, dk, dv, gate, col, row = outs[9:]
    dn = _deltanet_call(dq, dk, dv, col, row, gate, dn_out_norm)
    return _ffn_call(x, att, dn, w_out, norm2, w_up, ffn_conv_w, ffn_conv_b, w_down)


def kernel(x_prompt, x_sample, norm1, w_in, att_q_norm, att_k_norm, dn_conv_w, dn_a_log, dn_dt_bias,
           dn_out_norm, w_out, norm2, w_up, ffn_conv_w, ffn_conv_b, w_down):
    def trunk(x):
        for l in range(norm1.shape[0]):
            x = _layer(x, norm1[l], w_in[l], att_q_norm[l], att_k_norm[l], dn_conv_w[l], dn_a_log[l],
                       dn_dt_bias[l], dn_out_norm[l], w_out[l], norm2[l], w_up[l], ffn_conv_w[l],
                       ffn_conv_b[l], w_down[l])
        return x

    return trunk(x_prompt), trunk(x_sample)
```

```python
import functools
import math

import jax
import jax.numpy as jnp
from jax import lax
from jax.experimental import pallas as pl
from jax.experimental.pallas import tpu as pltpu

D_MODEL = 1024
ATT_HEADS = 8
ATT_HEAD_DIM = 64
ATT_WIDTH = ATT_HEADS * ATT_HEAD_DIM
DN_HEADS = 4
DN_HEAD_DIM = 128
DN_WIDTH = DN_HEADS * DN_HEAD_DIM
DN_CONV = 5
DN_CHUNK = 64
FFN_DIM = 2816
FFN_CONV = 3
NORM_EPS = 1e-6

LANES = 128
SUBLANES = 8
HALO_ROWS = 16
N_SLABS = (ATT_WIDTH + DN_WIDTH) // LANES
FFN_CHUNK = 256
N_FFN_CHUNKS = FFN_DIM // FFN_CHUNK
VMEM_LIMIT = 56 * 1024 * 1024

F32 = jnp.float32
BF16 = jnp.bfloat16


def _rms_scale(xf):
    return lax.rsqrt(jnp.mean(xf * xf, axis=-1, keepdims=True) + NORM_EPS)


def _silu(x):
    return x * (1.0 / (1.0 + jnp.exp(-x)))


def _const_spec(shape):
    nd = len(shape)
    return pl.BlockSpec(shape, lambda b, i: (0,) * nd, pipeline_mode=pl.Buffered(1))


ATT_SLABS = ATT_WIDTH // LANES
DILATIONS = (1, 4, 16)
ROPE_HALF = ATT_HEAD_DIM // 2


def _inproj_kernel(x_ref, xp_ref, xn_ref, norm1_ref, wa_ref, wd_ref, wg_ref, wba_ref, qkg_ref,
                   cos_ref, sin_ref, convw_ref, gpar_ref,
                   q1_ref, q4_ref, q16_ref, k1_ref, k4_ref, k16_ref, v1_ref, v4_ref, v16_ref,
                   dq_ref, dk_ref, dv_ref, gate_ref, col_ref, row_ref,
                   row_scr, perm_scr, *, tm):
    i = pl.program_id(1)
    last = pl.num_programs(1) - 1
    gain1 = norm1_ref[...]

    def normed(x):
        return x * _rms_scale(x) * gain1

    nf = normed(x_ref[0])
    n = nf.astype(BF16)
    nh = normed(jnp.concatenate([xp_ref[0, HALO_ROWS - 8:, :], xn_ref[0, :8, :]], axis=0)).astype(BF16)
    run = tm // SUBLANES
    for s in range(SUBLANES):
        for sl in range(D_MODEL // LANES):
            row_scr[sl, pl.ds(s, run, stride=SUBLANES), :] = nf[s * run:(s + 1) * run,
                                                               sl * LANES:(sl + 1) * LANES]
    n_rows = jnp.concatenate([row_scr[sl] for sl in range(D_MODEL // LANES)], axis=-1).astype(BF16)

    att = jnp.dot(n, wa_ref[...], preferred_element_type=F32)
    dn_all = jnp.dot(jnp.concatenate([n_rows, nh], axis=0), wd_ref[...], preferred_element_type=F32)
    g = jnp.dot(n, wg_ref[...], preferred_element_type=F32)
    ba = jnp.dot(n, wba_ref[...], preferred_element_type=F32)

    lane = lax.broadcasted_iota(jnp.int32, (1, LANES), 1)
    head_a = lane < ATT_HEAD_DIM
    first_half = (lane % ATT_HEAD_DIM) < ROPE_HALF
    cos = cos_ref[...]
    sin = sin_ref[...]
    for s in range(ATT_SLABS):
        for which in range(2):
            t = att[:, which * ATT_WIDTH + s * LANES: which * ATT_WIDTH + (s + 1) * LANES]
            ss = t * t
            sa = jnp.sum(jnp.where(head_a, ss, 0.0), axis=-1, keepdims=True)
            sb = jnp.sum(jnp.where(head_a, 0.0, ss), axis=-1, keepdims=True)
            ms = jnp.where(head_a, sa, sb) * (1.0 / ATT_HEAD_DIM)
            tn = t * lax.rsqrt(ms + NORM_EPS) * qkg_ref[which:which + 1, :]
            rot = jnp.where(first_half, pltpu.roll(tn, LANES - ROPE_HALF, 1), pltpu.roll(tn, ROPE_HALF, 1))
            r = tn * cos + rot * sin
            if which == 0:
                r = r * (1.0 / math.sqrt(ATT_HEAD_DIM))
            perm_scr[which * ATT_SLABS + s] = r
        perm_scr[2 * ATT_SLABS + s] = att[:, 2 * ATT_WIDTH + s * LANES: 2 * ATT_WIDTH + (s + 1) * LANES]
    outs = ((q1_ref, q4_ref, q16_ref), (k1_ref, k4_ref, k16_ref), (v1_ref, v4_ref, v16_ref))
    for which in range(3):
        o1, o4, o16 = outs[which]
        for s in range(ATT_SLABS):
            slab = which * ATT_SLABS + s
            o1[0, s] = perm_scr[slab].astype(BF16)
            for r in range(4):
                o4[0, r, s] = perm_scr[slab, pl.ds(r, tm // 4, stride=4), :].astype(BF16)
            for r in range(16):
                o16[0, r, s] = perm_scr[slab, pl.ds(r, tm // 16, stride=16), :].astype(BF16)

    keep_prev = (i > 0).astype(F32)
    keep_next = (i < last).astype(F32)
    sub = lax.broadcasted_iota(jnp.int32, (SUBLANES, 1), 0)
    dn = dn_all[:tm]

    def edge_before(vreg_rows, token_row):
        return jnp.where(sub == 0, dn_all[token_row:token_row + 1] * keep_prev,
                         pltpu.roll(dn[vreg_rows], 1, 0))

    def edge_after(vreg_rows, token_row):
        return jnp.where(sub == SUBLANES - 1, dn_all[token_row:token_row + 1] * keep_next,
                         pltpu.roll(dn[vreg_rows], SUBLANES - 1, 0))

    b1 = edge_before(slice(tm - 8, tm), tm + 7)
    b2 = edge_before(slice(tm - 16, tm - 8), tm + 6)
    a1 = edge_after(slice(0, 8), tm + 8)
    a2 = edge_after(slice(8, 16), tm + 9)
    cat = jnp.concatenate
    cw = convw_ref[...]
    y = (cat([b2, b1, dn[:tm - 16]], axis=0) * cw[0:1] + cat([b1, dn[:tm - 8]], axis=0) * cw[1:2]
         + dn * cw[2:3]
         + cat([dn[8:], a1], axis=0) * cw[3:4] + cat([dn[16:], a1, a2], axis=0) * cw[4:5])
    y = _silu(y)
    for h in range(DN_HEADS):
        qh = y[:, h * LANES:(h + 1) * LANES]
        kh = y[:, DN_WIDTH + h * LANES: DN_WIDTH + (h + 1) * LANES]
        vh = y[:, 2 * DN_WIDTH + h * LANES: 2 * DN_WIDTH + (h + 1) * LANES]
        qh = qh * lax.rsqrt(jnp.sum(qh * qh, axis=-1, keepdims=True) + NORM_EPS) * (DN_HEAD_DIM ** -0.5)
        kh = kh * lax.rsqrt(jnp.sum(kh * kh, axis=-1, keepdims=True) + NORM_EPS)
        for j, (val, ref) in enumerate(((qh, dq_ref), (kh, dk_ref), (vh, dv_ref))):
            row_scr[j] = val
            for s in range(SUBLANES):
                ref[0, h, s * run:(s + 1) * run, :] = row_scr[j, pl.ds(s, run, stride=SUBLANES), :].astype(BF16)
    for h in range(DN_HEADS):
        gate_ref[0, h] = _silu(g[:, h * LANES:(h + 1) * LANES]).astype(BF16)

    gp = gpar_ref[...]
    beta = 1.0 / (1.0 + jnp.exp(-ba))
    z = ba + gp[0:1]
    softplus = jnp.maximum(z, 0.0) + jnp.log(1.0 + jnp.exp(-jnp.abs(z)))
    colv = jnp.where(lane < 2 * DN_HEADS, beta, -jnp.exp(gp[1:2]) * softplus)
    t = colv.T
    gl = t[8:16]
    pos = lax.broadcasted_iota(jnp.int32, (1, tm), 1) % DN_CHUNK
    pre, suf = gl, gl
    step = 1
    while step < DN_CHUNK:
        pre = pre + jnp.where(pos >= step, pltpu.roll(pre, step, 1), 0.0)
        suf = suf + jnp.where(pos < DN_CHUNK - step, pltpu.roll(suf, tm - step, 1), 0.0)
        step *= 2
    sub = lax.broadcasted_iota(jnp.int32, (8, 1), 0)
    gc = jnp.where(sub < DN_HEADS, pre, suf)
    row_ref[0] = jnp.concatenate([t[0:8], gc], axis=0)
    col_ref[0] = jnp.concatenate([t[0:8], gc, t[16:]], axis=0).T


def _rope_tables(length):
    inv_freq = 1.0 / (10000.0 ** (jnp.arange(ROPE_HALF, dtype=F32) * 2.0 / ATT_HEAD_DIM))
    ang = jnp.arange(length, dtype=F32)[:, None] * inv_freq[None, :]
    cos, sin = jnp.cos(ang), jnp.sin(ang)
    cos128 = jnp.tile(cos, (1, LANES // ROPE_HALF))
    sin128 = jnp.tile(jnp.concatenate([-sin, sin], axis=-1), (1, LANES // ATT_HEAD_DIM))
    return cos128, sin128


def _inproj_call(x, norm1, w_in, att_q_norm, att_k_norm, dn_conv_w, dn_a_log, dn_dt_bias, *, tm=512):
    bsz, length, _ = x.shape
    nt = length // tm
    hb = tm // HALO_ROWS
    nhb = length // HALO_ROWS
    o3 = 3 * ATT_WIDTH
    o4 = o3 + 3 * DN_WIDTH
    o5 = o4 + DN_WIDTH
    wa = w_in[:, :o3].astype(BF16)
    wd = w_in[:, o3:o4].astype(BF16)
    wg = w_in[:, o4:o5].astype(BF16)
    wba = jnp.pad(w_in[:, o5:], ((0, 0), (0, LANES - 4 * DN_HEADS))).astype(BF16)
    qkg = jnp.stack([jnp.tile(att_q_norm, 2), jnp.tile(att_k_norm, 2)]
                    + [jnp.zeros((LANES,), F32)] * 6)
    convw = jnp.concatenate([dn_conv_w, jnp.zeros((3, 3 * DN_WIDTH), F32)], axis=0)
    pad_lanes = lambda v: jnp.pad(v.reshape(-1), (2 * DN_HEADS, LANES - 4 * DN_HEADS))
    gpar = jnp.stack([pad_lanes(dn_dt_bias), pad_lanes(dn_a_log)] + [jnp.zeros((LANES,), F32)] * 6)
    cos128, sin128 = _rope_tables(length)

    prev = lambda b, i: (b, jnp.maximum(i * hb - 1, 0), 0)
    nxt = lambda b, i: (b, jnp.minimum((i + 1) * hb, nhb - 1), 0)
    slab = jax.ShapeDtypeStruct((bsz, ATT_SLABS, length, LANES), BF16)
    slab_spec = pl.BlockSpec((1, ATT_SLABS, tm, LANES), lambda b, i: (b, 0, i, 0))
    out_shape, out_specs = [], []
    for _ in range(3):
        for d in DILATIONS:
            if d == 1:
                out_shape.append(slab)
                out_specs.append(slab_spec)
            else:
                out_shape.append(jax.ShapeDtypeStruct((bsz, d, ATT_SLABS, length // d, LANES), BF16))
                out_specs.append(pl.BlockSpec((1, d, ATT_SLABS, tm // d, LANES),
                                              lambda b, i: (b, 0, 0, i, 0)))
    out_shape += [slab] * 4
    out_specs += [slab_spec] * 4
    out_shape += [jax.ShapeDtypeStruct((bsz, length, LANES), F32),
                  jax.ShapeDtypeStruct((bsz, 16, length), F32)]
    out_specs += [pl.BlockSpec((1, tm, LANES), lambda b, i: (b, i, 0)),
                  pl.BlockSpec((1, 16, tm), lambda b, i: (b, 0, i))]

    return pl.pallas_call(
        functools.partial(_inproj_kernel, tm=tm),
        grid=(bsz, nt),
        in_specs=[
            pl.BlockSpec((1, tm, D_MODEL), lambda b, i: (b, i, 0)),
            pl.BlockSpec((1, HALO_ROWS, D_MODEL), prev),
            pl.BlockSpec((1, HALO_ROWS, D_MODEL), nxt),
            _const_spec((1, D_MODEL)),
            _const_spec((D_MODEL, o3)),
            _const_spec((D_MODEL, 3 * DN_WIDTH)),
            _const_spec((D_MODEL, DN_WIDTH)),
            _const_spec((D_MODEL, LANES)),
            _const_spec((8, LANES)),
            pl.BlockSpec((tm, LANES), lambda b, i: (i, 0)),
            pl.BlockSpec((tm, LANES), lambda b, i: (i, 0)),
            _const_spec((8, 3 * DN_WIDTH)),
            _const_spec((8, LANES)),
        ],
        out_specs=out_specs,
        out_shape=out_shape,
        scratch_shapes=[pltpu.VMEM((D_MODEL // LANES, tm, LANES), F32),
                        pltpu.VMEM((3 * ATT_SLABS, tm, LANES), F32)],
        compiler_params=pltpu.CompilerParams(
            dimension_semantics=("parallel", "arbitrary"), vmem_limit_bytes=VMEM_LIMIT),
        name="inproj",
    )(x, x, x, norm1.reshape(1, D_MODEL), wa, wd, wg, wba, qkg, cos128, sin128, convw, gpar)


ATT_TILE = 1024
ATT_QB = 64
ATT_RADIUS = 64
ATT_KW = ATT_QB + 2 * ATT_RADIUS
ATT_GROUP = 2
NEG_INF = -1e30


def _attn_kernel(q1, k1, k1p, k1n, v1, v1p, v1n,
                 q4, k4, k4p, k4n, v4, v4p, v4n,
                 q16, k16, k16p, k16n, v16, v16p, v16n,
                 o_ref, acc_o, acc_m, bias_scr):
    i = pl.program_id(1)
    last = pl.num_programs(1) - 1
    lane = lax.broadcasted_iota(jnp.int32, (1, LANES), 1)
    is_a = lane < ATT_HEAD_DIM

    qi = lax.broadcasted_iota(jnp.int32, (ATT_QB, ATT_KW), 0)
    kj = lax.broadcasted_iota(jnp.int32, (ATT_QB, ATT_KW), 1)
    band = jnp.where((kj >= qi) & (kj <= qi + 2 * ATT_RADIUS), 0.0, NEG_INF)
    first_seq = (i == 0).astype(F32)
    last_seq = (i == last).astype(F32)
    before = jnp.where(kj < ATT_RADIUS, NEG_INF, 0.0)
    after = jnp.where(kj >= ATT_QB + ATT_RADIUS, NEG_INF, 0.0)
    bias_scr[0] = band
    bias_scr[1] = band + first_seq * before
    bias_scr[2] = band + last_seq * after
    bias_scr[3] = band + first_seq * before + last_seq * after

    def unit_group(stage, blocks):
        items = [(bi, s) for bi in range(len(blocks)) for s in range(ATT_SLABS)]
        heads = [(bi, s, hh) for bi, s in items for hh in range(2)]
        s2, vext = {}, {}
        for bi, s in items:
            _, q_get, k_get, v_get, _, _ = blocks[bi]
            qt = q_get(s)
            vw = v_get(s)
            zero = jnp.zeros_like(qt)
            qq = jnp.concatenate([jnp.where(is_a, qt, zero), jnp.where(is_a, zero, qt)], axis=0)
            s2[bi, s] = lax.dot_general(qq, k_get(s), (((1,), (1,)), ((), ())),
                                        preferred_element_type=F32)
            one = jnp.ones_like(vw)
            vext[bi, s] = (jnp.where(is_a, vw, one), jnp.where(is_a, one, vw))
        m_prev, m_new, p16 = {}, {}, {}
        for it in heads:
            bi, s, hh = it
            rows = blocks[bi][4]
            sc = s2[bi, s][hh * ATT_QB:(hh + 1) * ATT_QB] + bias_scr[blocks[bi][0]]
            m_cur = jnp.max(sc, axis=-1, keepdims=True)
            if stage == 0:
                m_new[it] = jnp.broadcast_to(m_cur, (ATT_QB, LANES))
            else:
                m_prev[it] = acc_m[2 * s + hh, rows, :]
                m_new[it] = jnp.maximum(m_prev[it], m_cur)
            p16[it] = jnp.concatenate(
                [jnp.exp(sc[:, :LANES] - m_new[it]),
                 jnp.exp(sc[:, LANES:] - m_new[it][:, :ATT_KW - LANES])], axis=-1).astype(BF16)
        o = {it: jnp.dot(p16[it], vext[it[0], it[1]][it[2]], preferred_element_type=F32)
             for it in heads}
        for it in heads:
            bi, s, hh = it
            rows = blocks[bi][4]
            if stage > 0:
                o[it] = o[it] + jnp.exp(m_prev[it] - m_new[it]) * acc_o[2 * s + hh, rows, :]
            if stage < 2:
                acc_m[2 * s + hh, rows, :] = m_new[it]
                acc_o[2 * s + hh, rows, :] = o[it]
        if stage == 2:
            for bi, s in items:
                norm = [o[bi, s, hh] * (1.0 / pltpu.roll(o[bi, s, hh], ATT_HEAD_DIM, 1)) for hh in range(2)]
                o_ref[0, s, blocks[bi][5], :] = jnp.where(is_a, norm[0], norm[1]).astype(BF16)

    def run_pattern(stage, d, q, k, kp, kn, v, vp, vn):
        ld = ATT_TILE // d
        nqb = ld // ATT_QB

        def at(ref, r, s, rows):
            return ref[0, s, rows, :] if d == 1 else ref[0, r, s, rows, :]

        def full(ref, r, s):
            return ref[0, s] if d == 1 else ref[0, r, s]

        def acc_rows(r, qb):
            start = qb * (ATT_QB * d) + r
            return pl.ds(start, ATT_QB) if d == 1 else pl.ds(start, ATT_QB, stride=d)

        def block(r, qb):
            q_rows = pl.ds(qb * ATT_QB, ATT_QB) if isinstance(qb, int) else \
                pl.ds(pl.multiple_of(qb * ATT_QB, ATT_QB), ATT_QB)
            q_get = lambda s: at(q, r, s, q_rows)
            if nqb == 1:
                pieces = lambda m, mp, mn: (lambda s: jnp.concatenate(
                    [full(mp, r, s), full(m, r, s), full(mn, r, s)], axis=0))
                return (3, q_get, pieces(k, kp, kn), pieces(v, vp, vn), acc_rows(r, 0), q_rows)
            if isinstance(qb, int) and qb == 0:
                head_rows = pl.ds(0, ATT_QB + ATT_RADIUS)
                pieces = lambda m, mp: (lambda s: jnp.concatenate(
                    [full(mp, r, s), at(m, r, s, head_rows)], axis=0))
                return (1, q_get, pieces(k, kp), pieces(v, vp), acc_rows(r, 0), q_rows)
            if isinstance(qb, int) and qb == nqb - 1:
                tail_rows = pl.ds(ld - ATT_QB - ATT_RADIUS, ATT_QB + ATT_RADIUS)
                pieces = lambda m, mn: (lambda s: jnp.concatenate(
                    [at(m, r, s, tail_rows), full(mn, r, s)], axis=0))
                return (2, q_get, pieces(k, kn), pieces(v, vn), acc_rows(r, qb), q_rows)
            start = qb * ATT_QB - ATT_RADIUS
            win = pl.ds(start if isinstance(qb, int) else pl.multiple_of(start, ATT_RADIUS), ATT_KW)
            return (0, q_get, lambda s: at(k, r, s, win), lambda s: at(v, r, s, win),
                    acc_rows(r, qb), q_rows)

        if nqb == 1:
            def pair(j, carry):
                unit_group(stage, [block(2 * j, 0), block(2 * j + 1, 0)])
                return carry
            lax.fori_loop(0, d // ATT_GROUP, pair, 0)
            return

        def residue(r, carry):
            unit_group(stage, [block(r, 0), block(r, 1)])

            def interior(j, c):
                unit_group(stage, [block(r, 2 * j), block(r, 2 * j + 1)])
                return c

            if nqb > 2 * ATT_GROUP:
                lax.fori_loop(1, nqb // ATT_GROUP - 1, interior, 0)
            unit_group(stage, [block(r, nqb - 2), block(r, nqb - 1)])
            return carry

        if d == 1:
            residue(0, 0)
        else:
            lax.fori_loop(0, d, residue, 0)

    run_pattern(0, 16, q16, k16, k16p, k16n, v16, v16p, v16n)
    run_pattern(1, 4, q4, k4, k4p, k4n, v4, v4p, v4n)
    run_pattern(2, 1, q1, k1, k1p, k1n, v1, v1p, v1n)


def _attn_call(qkv):
    q1, q4, q16, k1, k4, k16, v1, v4, v16 = qkv
    bsz, _, length, _ = q1.shape
    nt = length // ATT_TILE
    args, specs = [], []
    for d, q, k, v in ((1, q1, k1, v1), (4, q4, k4, v4), (16, q16, k16, v16)):
        ld = ATT_TILE // d
        hb = ld // ATT_RADIUS
        nhb = length // d // ATT_RADIUS
        if d == 1:
            main = pl.BlockSpec((1, ATT_SLABS, ld, LANES), lambda b, i: (b, 0, i, 0))
            prev = pl.BlockSpec((1, ATT_SLABS, ATT_RADIUS, LANES),
                                lambda b, i, hb=hb: (b, 0, jnp.maximum(i * hb - 1, 0), 0))
            nxt = pl.BlockSpec((1, ATT_SLABS, ATT_RADIUS, LANES),
                               lambda b, i, hb=hb, nhb=nhb: (b, 0, jnp.minimum((i + 1) * hb, nhb - 1), 0))
        else:
            main = pl.BlockSpec((1, d, ATT_SLABS, ld, LANES), lambda b, i: (b, 0, 0, i, 0))
            prev = pl.BlockSpec((1, d, ATT_SLABS, ATT_RADIUS, LANES),
                                lambda b, i, hb=hb: (b, 0, 0, jnp.maximum(i * hb - 1, 0), 0))
            nxt = pl.BlockSpec((1, d, ATT_SLABS, ATT_RADIUS, LANES),
                               lambda b, i, hb=hb, nhb=nhb: (b, 0, 0, jnp.minimum((i + 1) * hb, nhb - 1), 0))
        args += [q, k, k, k, v, v, v]
        specs += [main, main, prev, nxt, main, prev, nxt]
    return pl.pallas_call(
        _attn_kernel,
        grid=(bsz, nt),
        in_specs=specs,
        out_specs=pl.BlockSpec((1, ATT_SLABS, ATT_TILE, LANES), lambda b, i: (b, 0, i, 0)),
        out_shape=jax.ShapeDtypeStruct((bsz, ATT_SLABS, length, LANES), BF16),
        scratch_shapes=[pltpu.VMEM((ATT_HEADS, ATT_TILE, LANES), F32),
                        pltpu.VMEM((ATT_HEADS, ATT_TILE, LANES), F32),
                        pltpu.VMEM((4, ATT_QB, ATT_KW), F32)],
        compiler_params=pltpu.CompilerParams(
            dimension_semantics=("parallel", "arbitrary"), vmem_limit_bytes=VMEM_LIMIT),
        name="dilated_attention",
    )(*args)


DN_TILE = 256
DN_ROWS = 2
DN_GROUP = 4


def _dot_nt(a, b, **kw):
    return lax.dot_general(a, b, (((1,), (1,)), ((), ())), preferred_element_type=F32, **kw)


def _deltanet_kernel(*refs, reverse, tile, rows_per_step):
    if reverse:
        dq_ref, dk_ref, dv_ref, col_ref, row_ref, o_ref, state = refs
    else:
        dq_ref, dk_ref, dv_ref, col_ref, row_ref, gate_ref, obwd_ref, gain_ref, o_ref, state = refs
    c_sz = DN_CHUNK
    nchunks = tile // c_sz

    @pl.when(pl.program_id(1) == 0)
    def _():
        state[...] = jnp.zeros_like(state)

    ii = lax.broadcasted_iota(jnp.int32, (c_sz, 2 * c_sz), 0)
    jj = lax.broadcasted_iota(jnp.int32, (c_sz, 2 * c_sz), 1) % c_sz
    incl = (jj >= ii) if reverse else (jj <= ii)
    strict = (jj > ii) if reverse else (jj < ii)
    edge = 0 if reverse else c_sz - 1
    lo = lax.broadcasted_iota(jnp.int32, (1, 2 * c_sz), 1) < c_sz

    order = list(range(nchunks - 1, -1, -1)) if reverse else list(range(nchunks))
    rows_of = lambda c: slice(c * c_sz, (c + 1) * c_sz)
    bf = lambda t: t.astype(BF16)
    dot = functools.partial(jnp.dot, preferred_element_type=F32)
    cat = jnp.concatenate

    def blockdiag(t16):
        zero = jnp.zeros_like(t16)
        return cat([jnp.where(lo, t16, zero), jnp.where(lo, zero, t16)], axis=0)

    pre = {}
    for g0 in range(0, nchunks, DN_GROUP):
        units = [(c, b, hp) for c in order[g0:g0 + DN_GROUP] for b in range(rows_per_step)
                 for hp in range(DN_HEADS // 2)]
        colt = {(c, b): col_ref[b, rows_of(c), :] for c, b, _ in units}
        rowt = {(c, b): row_ref[b, :, rows_of(c)] for c, b, _ in units}
        v = {}
        for un in units:
            c, b, hp = un
            per_head = []
            for h in (2 * hp, 2 * hp + 1):
                idx = (DN_HEADS if reverse else 0) + h
                ct = colt[c, b]
                beta = ct[:, idx:idx + 1]
                gcc = ct[:, 2 * DN_HEADS + idx:2 * DN_HEADS + idx + 1]
                gcr = rowt[c, b][2 * DN_HEADS + idx:2 * DN_HEADS + idx + 1, :]
                gl = gcc[edge:edge + 1, :]
                k16 = dk_ref[b, h, rows_of(c), :]
                q16 = dq_ref[b, h, rows_of(c), :]
                kf = k16.astype(F32)
                kbeta = kf * beta
                egc = jnp.exp(gcc)
                vf = dv_ref[b, h, rows_of(c), :].astype(F32)
                per_head.append(dict(
                    gcc=gcc, gcr=gcr, k16=k16, q16=q16, kb16=bf(kbeta), eg=jnp.exp(gl),
                    rhs=bf(cat([vf * beta, kbeta * egc], axis=-1)), rhs32=(vf * beta, kbeta * egc),
                    qd=bf(q16.astype(F32) * egc), kd=bf(kf * jnp.exp(gl - gcc))))
            a, bb = per_head
            zero = jnp.zeros_like(a["k16"])
            k_diag = cat([cat([a["k16"], zero], axis=1), cat([zero, bb["k16"]], axis=1)], axis=0)
            gdiff = (jnp.where(lo, a["gcc"], bb["gcc"]) - cat([a["gcr"], bb["gcr"]], axis=1))
            decay = jnp.exp(jnp.where(incl, gdiff, NEG_INF))
            v[un] = dict(heads=per_head, decay=decay,
                         kk=_dot_nt(cat([a["kb16"], bb["kb16"]], axis=1), k_diag),
                         qk=_dot_nt(cat([a["q16"], bb["q16"]], axis=1), k_diag))
        mpow = {un: -jnp.where(strict, v[un]["kk"] * v[un]["decay"], 0.0) for un in units}
        tlow = dict(mpow)
        for _ in range(5):
            m16 = {un: bf(mpow[un]) for un in units}
            mpow = {un: dot(m16[un], blockdiag(m16[un])) for un in units}
            tlow = {un: tlow[un] + mpow[un] + dot(bf(tlow[un]), blockdiag(bf(mpow[un]))) for un in units}
        for un in units:
            c, b, hp = un
            a, bb = v[un]["heads"]
            t16 = bf(tlow[un])
            zero = jnp.zeros_like(t16)
            rhs2 = cat([a["rhs"], bb["rhs"]], axis=0)
            qk16 = bf(v[un]["qk"] * v[un]["decay"])
            for hh, hd in enumerate((a, bb)):
                mine = lo if hh == 0 else jnp.logical_not(lo)
                sol = dot(jnp.where(mine, t16, zero), rhs2)
                pre[c, b, 2 * hp + hh] = dict(
                    u=hd["rhs32"][0] + sol[:, :DN_HEAD_DIM],
                    wq=cat([bf(hd["rhs32"][1] + sol[:, DN_HEAD_DIM:]), hd["qd"]], axis=0),
                    qk=jnp.where(mine, qk16, jnp.zeros_like(qk16)), kd=hd["kd"], eg=hd["eg"])

    chains = [(b, h) for b in range(rows_per_step) for h in range(DN_HEADS)]
    for c in order:
        st = {n: state[n[0] * DN_HEADS + n[1]] for n in chains}
        st16 = {n: bf(st[n]) for n in chains}
        ws = {n: dot(pre[(c,) + n]["wq"], st16[n]) for n in chains}
        v16 = {n: bf(pre[(c,) + n]["u"] - ws[n][:c_sz]) for n in chains}
        vpair = {(b, h // 2): cat([v16[b, h - h % 2], v16[b, h - h % 2 + 1]], axis=0) for b, h in chains}
        ov = {n: dot(pre[(c,) + n]["qk"], vpair[n[0], n[1] // 2]) for n in chains}
        sv = {n: lax.dot_general(pre[(c,) + n]["kd"], v16[n], (((0,), (0,)), ((), ())),
                                 preferred_element_type=F32) for n in chains}
        for n in chains:
            b, h = n
            state[b * DN_HEADS + h] = st[n] * pre[(c,) + n]["eg"] + sv[n]
            o = ws[n][c_sz:] + ov[n]
            if reverse:
                o_ref[b, h, rows_of(c), :] = o
            else:
                tot = o + obwd_ref[b, h, rows_of(c), :]
                y = tot * _rms_scale(tot) * gain_ref[...] * gate_ref[b, h, rows_of(c), :].astype(F32)
                o_ref[b, h, rows_of(c), :] = y.astype(BF16)


def _deltanet_call(dq, dk, dv, col, row, gate, dn_out_norm, *, tile=DN_TILE, rows_per_step=DN_ROWS):
    bsz, _, length, _ = dq.shape
    assert bsz % rows_per_step == 0 and length % tile == 0
    nt = length // tile
    nb = bsz // rows_per_step

    def run(reverse, extra_args, extra_specs, out_dtype):
        pos = (lambda i: nt - 1 - i) if reverse else (lambda i: i)
        slab = pl.BlockSpec((rows_per_step, DN_HEADS, tile, LANES), lambda b, i: (b, 0, pos(i), 0))
        return pl.pallas_call(
            functools.partial(_deltanet_kernel, reverse=reverse, tile=tile, rows_per_step=rows_per_step),
            grid=(nb, nt),
            in_specs=[slab, slab, slab,
                      pl.BlockSpec((rows_per_step, tile, LANES), lambda b, i: (b, pos(i), 0)),
                      pl.BlockSpec((rows_per_step, 16, tile), lambda b, i: (b, 0, pos(i)))] + extra_specs(slab),
            out_specs=slab,
            out_shape=jax.ShapeDtypeStruct((bsz, DN_HEADS, length, LANES), out_dtype),
            scratch_shapes=[pltpu.VMEM((rows_per_step * DN_HEADS, DN_HEAD_DIM, DN_HEAD_DIM), F32)],
            compiler_params=pltpu.CompilerParams(
                dimension_semantics=("parallel", "arbitrary"), vmem_limit_bytes=VMEM_LIMIT),
            name="deltanet_bwd" if reverse else "deltanet_fwd",
        )(dq, dk, dv, col, row, *extra_args)

    o_bwd = run(True, (), lambda slab: [], F32)
    return run(False, (gate, o_bwd, dn_out_norm.reshape(1, DN_HEAD_DIM)),
               lambda slab: [slab, slab, pl.BlockSpec((1, DN_HEAD_DIM), lambda b, i: (0, 0))], BF16)


def _ffn_kernel(x_ref, xp_ref, xn_ref, att_ref, attp_ref, attn_ref, dn_ref, dnp_ref, dnn_ref,
                wout_ref, norm2_ref, wg_ref, wu_ref, cg_ref, cu_ref, wd_ref, o_ref, perm_scr, *, tm):
    i = pl.program_id(1)
    last = pl.num_programs(1) - 1
    gain = norm2_ref[...]
    run = tm // SUBLANES

    half = N_SLABS // 2
    xh = jnp.concatenate([xp_ref[0, HALO_ROWS - 8:, :], xn_ref[0, :8, :]], axis=0)
    mh = ([jnp.concatenate([attp_ref[0, s], attn_ref[0, s]], axis=0) for s in range(half)]
          + [jnp.concatenate([dnp_ref[0, s], dnn_ref[0, s]], axis=0) for s in range(half)])
    mh = [jnp.concatenate([m[HALO_ROWS - 8:HALO_ROWS], m[HALO_ROWS:HALO_ROWS + 8]], axis=0) for m in mh]
    main = [att_ref[0, s] for s in range(half)] + [dn_ref[0, s] for s in range(half)]
    mix = jnp.concatenate([jnp.concatenate([a, b], axis=0) for a, b in zip(main, mh)], axis=-1)
    h_all = (jnp.concatenate([x_ref[0], xh], axis=0)
             + jnp.dot(mix, wout_ref[...], preferred_element_type=F32))
    n_all = h_all * _rms_scale(h_all) * gain
    h = h_all[:tm]

    for s in range(SUBLANES):
        for sl in range(N_SLABS):
            perm_scr[sl, pl.ds(s, run, stride=SUBLANES), :] = n_all[s * run:(s + 1) * run,
                                                                    sl * LANES:(sl + 1) * LANES]
    n2_all = jnp.concatenate([jnp.concatenate([perm_scr[sl] for sl in range(N_SLABS)], axis=-1),
                              n_all[tm:]], axis=0).astype(BF16)
    sub = lax.broadcasted_iota(jnp.int32, (SUBLANES, 1), 0)
    keep_prev = (i > 0).astype(F32)
    keep_next = (i < last).astype(F32)

    def up_proj(c):
        return [jnp.dot(n2_all, w_ref[c], preferred_element_type=F32) for w_ref in (wg_ref, wu_ref)]

    def conv(c, y_all, cw_ref):
        y = y_all[:tm]
        before = y_all[tm + 7:tm + 8] * keep_prev
        after = y_all[tm + 8:tm + 9] * keep_next
        y_prev0 = jnp.where(sub == 0, before, pltpu.roll(y[tm - SUBLANES:], 1, 0))
        y_next_last = jnp.where(sub == SUBLANES - 1, after, pltpu.roll(y[:SUBLANES], SUBLANES - 1, 0))
        y_prev = jnp.concatenate([y_prev0, y[:tm - SUBLANES]], axis=0)
        y_next = jnp.concatenate([y[SUBLANES:], y_next_last], axis=0)
        cw = cw_ref[c]
        return y_prev * cw[0:1] + y * cw[1:2] + y_next * cw[2:3] + cw[3:4]

    acc = jnp.zeros((tm, D_MODEL), F32)
    ahead = up_proj(0)
    for c in range(N_FFN_CHUNKS):
        yg, yu = ahead
        if c + 1 < N_FFN_CHUNKS:
            ahead = up_proj(c + 1)
        act = (_silu(conv(c, yg, cg_ref)) * conv(c, yu, cu_ref)).astype(BF16)
        acc = acc + jnp.dot(act, wd_ref[c], preferred_element_type=F32)
    for sl in range(N_SLABS):
        perm_scr[sl] = acc[:, sl * LANES:(sl + 1) * LANES]
    for s in range(SUBLANES):
        for sl in range(N_SLABS):
            o_ref[0, s * run:(s + 1) * run, sl * LANES:(sl + 1) * LANES] = (
                h[s * run:(s + 1) * run, sl * LANES:(sl + 1) * LANES]
                + perm_scr[sl, pl.ds(s, run, stride=SUBLANES), :])


def _ffn_call(x, att, dn, w_out, norm2, w_up, ffn_conv_w, ffn_conv_b, w_down, *, tm=256):
    bsz, length, _ = x.shape
    nt = length // tm
    hb = tm // HALO_ROWS
    nhb = length // HALO_ROWS

    def chunked_cols(w):
        return w.reshape(w.shape[0], N_FFN_CHUNKS, FFN_CHUNK).transpose(1, 0, 2)

    wg = chunked_cols(w_up[:, :FFN_DIM]).astype(BF16)
    wu = chunked_cols(w_up[:, FFN_DIM:]).astype(BF16)
    wd = w_down.reshape(N_FFN_CHUNKS, FFN_CHUNK, D_MODEL).astype(BF16)

    def conv_table(lo):
        t = jnp.concatenate([ffn_conv_w[:, lo:lo + FFN_DIM], ffn_conv_b[None, lo:lo + FFN_DIM],
                             jnp.zeros((4, FFN_DIM), F32)], axis=0)
        return chunked_cols(t)

    cg, cu = conv_table(0), conv_table(FFN_DIM)

    prev = lambda b, i: (b, jnp.maximum(i * hb - 1, 0), 0)
    nxt = lambda b, i: (b, jnp.minimum((i + 1) * hb, nhb - 1), 0)
    prev4 = lambda b, i: (b, 0, jnp.maximum(i * hb - 1, 0), 0)
    nxt4 = lambda b, i: (b, 0, jnp.minimum((i + 1) * hb, nhb - 1), 0)

    return pl.pallas_call(
        functools.partial(_ffn_kernel, tm=tm),
        grid=(bsz, nt),
        in_specs=[
            pl.BlockSpec((1, tm, D_MODEL), lambda b, i: (b, i, 0)),
            pl.BlockSpec((1, HALO_ROWS, D_MODEL), prev),
            pl.BlockSpec((1, HALO_ROWS, D_MODEL), nxt),
            pl.BlockSpec((1, N_SLABS // 2, tm, LANES), lambda b, i: (b, 0, i, 0)),
            pl.BlockSpec((1, N_SLABS // 2, HALO_ROWS, LANES), prev4),
            pl.BlockSpec((1, N_SLABS // 2, HALO_ROWS, LANES), nxt4),
            pl.BlockSpec((1, N_SLABS // 2, tm, LANES), lambda b, i: (b, 0, i, 0)),
            pl.BlockSpec((1, N_SLABS // 2, HALO_ROWS, LANES), prev4),
            pl.BlockSpec((1, N_SLABS // 2, HALO_ROWS, LANES), nxt4),
            _const_spec((D_MODEL, D_MODEL)),
            _const_spec((1, D_MODEL)),
            _const_spec((N_FFN_CHUNKS, D_MODEL, FFN_CHUNK)),
            _const_spec((N_FFN_CHUNKS, D_MODEL, FFN_CHUNK)),
            _const_spec((N_FFN_CHUNKS, 8, FFN_CHUNK)),
            _const_spec((N_FFN_CHUNKS, 8, FFN_CHUNK)),
            _const_spec((N_FFN_CHUNKS, FFN_CHUNK, D_MODEL)),
        ],
        out_specs=pl.BlockSpec((1, tm, D_MODEL), lambda b, i: (b, i, 0)),
        out_shape=jax.ShapeDtypeStruct((bsz, length, D_MODEL), F32),
        scratch_shapes=[pltpu.VMEM((N_SLABS, tm, LANES), F32)],
        compiler_params=pltpu.CompilerParams(
            dimension_semantics=("parallel", "arbitrary"), vmem_limit_bytes=VMEM_LIMIT),
        name="outproj_convglu",
    )(x, x, x, att, att, att, dn, dn, dn, w_out.astype(BF16), norm2.reshape(1, D_MODEL),
      wg, wu, cg, cu, wd)


def _layer(x, norm1, w_in, att_q_norm, att_k_norm, dn_conv_w, dn_a_log, dn_dt_bias, dn_out_norm,
           w_out, norm2, w_up, ffn_conv_w, ffn_conv_b, w_down):
    outs = _inproj_call(x, norm1, w_in, att_q_norm, att_k_norm, dn_conv_w, dn_a_log, dn_dt_bias)
    att = _attn_call(outs[:9])
    dq, dk, dv, gate, col, row = outs[9:]
    dn = _deltanet_call(dq, dk, dv, col, row, gate, dn_out_norm)
    return _ffn_call(x, att, dn, w_out, norm2, w_up, ffn_conv_w, ffn_conv_b, w_down)


def kernel(x_prompt, x_sample, norm1, w_in, att_q_norm, att_k_norm, dn_conv_w, dn_a_log, dn_dt_bias,
           dn_out_norm, w_out, norm2, w_up, ffn_conv_w, ffn_conv_b, w_down):
    def trunk(x):
        for l in range(norm1.shape[0]):
            x = _layer(x, norm1[l], w_in[l], att_q_norm[l], att_k_norm[l], dn_conv_w[l], dn_a_log[l],
                       dn_dt_bias[l], dn_out_norm[l], w_out[l], norm2[l], w_up[l], ffn_conv_w[l],
                       ffn_conv_b[l], w_down[l])
        return x

    return trunk(x_prompt), trunk(x_sample)
```

```python
import functools
import math

import jax
import jax.numpy as jnp
from jax import lax
from jax.experimental import pallas as pl
from jax.experimental.pallas import tpu as pltpu

D_MODEL = 1024
ATT_HEADS = 8
ATT_HEAD_DIM = 64
ATT_WIDTH = ATT_HEADS * ATT_HEAD_DIM
DN_HEADS = 4
DN_HEAD_DIM = 128
DN_WIDTH = DN_HEADS * DN_HEAD_DIM
DN_CONV = 5
DN_CHUNK = 64
FFN_DIM = 2816
FFN_CONV = 3
NORM_EPS = 1e-6

LANES = 128
SUBLANES = 8
HALO_ROWS = 16
N_SLABS = (ATT_WIDTH + DN_WIDTH) // LANES
FFN_CHUNK = 256
N_FFN_CHUNKS = FFN_DIM // FFN_CHUNK
VMEM_LIMIT = 56 * 1024 * 1024

F32 = jnp.float32
BF16 = jnp.bfloat16


def _rms_scale(xf):
    return lax.rsqrt(jnp.mean(xf * xf, axis=-1, keepdims=True) + NORM_EPS)


def _silu(x):
    return x * (1.0 / (1.0 + jnp.exp(-x)))


def _const_spec(shape):
    nd = len(shape)
    return pl.BlockSpec(shape, lambda b, i: (0,) * nd, pipeline_mode=pl.Buffered(1))


ATT_SLABS = ATT_WIDTH // LANES
DILATIONS = (1, 4, 16)
ROPE_HALF = ATT_HEAD_DIM // 2
ATT_Q_SCALE = math.log2(math.e) / math.sqrt(ATT_HEAD_DIM)


def _inproj_kernel(x_ref, xp_ref, xn_ref, norm1_ref, wa_ref, wd_ref, wg_ref, wba_ref, qkg_ref,
                   cos_ref, sin_ref, convw_ref, gpar_ref,
                   q1_ref, q4_ref, q16_ref, k1_ref, k4_ref, k16_ref, v1_ref, v4_ref, v16_ref,
                   dq_ref, dk_ref, dv_ref, gate_ref, col_ref, row_ref,
                   row_scr, perm_scr, mod4_scr, *, tm):
    i = pl.program_id(1)
    last = pl.num_programs(1) - 1
    gain1 = norm1_ref[...]

    def normed(x):
        return x * _rms_scale(x) * gain1

    nf = normed(x_ref[0])
    n = nf.astype(BF16)
    nh = normed(jnp.concatenate([xp_ref[0, HALO_ROWS - 8:, :], xn_ref[0, :8, :]], axis=0)).astype(BF16)
    run = tm // SUBLANES
    for s in range(SUBLANES):
        for sl in range(D_MODEL // LANES):
            row_scr[sl, pl.ds(s, run, stride=SUBLANES), :] = nf[s * run:(s + 1) * run,
                                                               sl * LANES:(sl + 1) * LANES]
    n_rows = jnp.concatenate([row_scr[sl] for sl in range(D_MODEL // LANES)], axis=-1).astype(BF16)

    att = jnp.dot(n, wa_ref[...], preferred_element_type=F32)
    dn_all = jnp.dot(jnp.concatenate([n_rows, nh], axis=0), wd_ref[...], preferred_element_type=F32)
    g = jnp.dot(n, wg_ref[...], preferred_element_type=F32)
    ba = jnp.dot(n, wba_ref[...], preferred_element_type=F32)

    lane = lax.broadcasted_iota(jnp.int32, (1, LANES), 1)
    gp = gpar_ref[...]
    beta = 1.0 / (1.0 + jnp.exp(-ba))
    z = ba + gp[0:1]
    softplus = jnp.maximum(z, 0.0) + jnp.log(1.0 + jnp.exp(-jnp.abs(z)))
    colv = jnp.where(lane < 2 * DN_HEADS, beta, -jnp.exp(gp[1:2]) * softplus)
    t = colv.T
    gl = t[8:16]
    pos = lax.broadcasted_iota(jnp.int32, (1, tm), 1) % DN_CHUNK
    pre, suf = gl, gl
    step = 1
    while step < DN_CHUNK:
        pre = pre + jnp.where(pos >= step, pltpu.roll(pre, step, 1), 0.0)
        suf = suf + jnp.where(pos < DN_CHUNK - step, pltpu.roll(suf, tm - step, 1), 0.0)
        step *= 2
    sub = lax.broadcasted_iota(jnp.int32, (SUBLANES, 1), 0)
    gc = jnp.where(sub < DN_HEADS, pre, suf)
    row_ref[0] = jnp.concatenate([t[0:8], gc], axis=0)
    col_ref[0] = jnp.concatenate([t[0:8], gc, t[16:]], axis=0).T

    head_a = lane < ATT_HEAD_DIM
    first_half = (lane % ATT_HEAD_DIM) < ROPE_HALF
    cos = cos_ref[...]
    sin = sin_ref[...]
    for s in range(ATT_SLABS):
        for which in range(2):
            t = att[:, which * ATT_WIDTH + s * LANES: which * ATT_WIDTH + (s + 1) * LANES]
            ss = t * t
            sa = jnp.sum(jnp.where(head_a, ss, 0.0), axis=-1, keepdims=True)
            sb = jnp.sum(jnp.where(head_a, 0.0, ss), axis=-1, keepdims=True)
            ms = jnp.where(head_a, sa, sb) * (1.0 / ATT_HEAD_DIM)
            tn = t * lax.rsqrt(ms + NORM_EPS) * qkg_ref[which:which + 1, :]
            rot = jnp.where(first_half, pltpu.roll(tn, LANES - ROPE_HALF, 1), pltpu.roll(tn, ROPE_HALF, 1))
            r = tn * cos + rot * sin
            if which == 0:
                r = r * ATT_Q_SCALE
            perm_scr[which * ATT_SLABS + s] = r
        perm_scr[2 * ATT_SLABS + s] = att[:, 2 * ATT_WIDTH + s * LANES: 2 * ATT_WIDTH + (s + 1) * LANES]
    outs = ((q1_ref, q4_ref, q16_ref), (k1_ref, k4_ref, k16_ref), (v1_ref, v4_ref, v16_ref))
    for which in range(3):
        o1, o4, o16 = outs[which]
        for s in range(ATT_SLABS):
            slab = which * ATT_SLABS + s
            o1[0, s] = perm_scr[slab].astype(BF16)
            for r in range(4):
                quarter = perm_scr[slab, pl.ds(r, tm // 4, stride=4), :]
                o4[0, r, s] = quarter.astype(BF16)
                mod4_scr[r] = quarter
            for r in range(4):
                for r2 in range(4):
                    o16[0, r + 4 * r2, s] = mod4_scr[r, pl.ds(r2, tm // 16, stride=4), :].astype(BF16)

    keep_prev = (i > 0).astype(F32)
    keep_next = (i < last).astype(F32)
    sub = lax.broadcasted_iota(jnp.int32, (SUBLANES, 1), 0)
    dn = dn_all[:tm]

    def edge_before(vreg_rows, token_row):
        return jnp.where(sub == 0, dn_all[token_row:token_row + 1] * keep_prev,
                         pltpu.roll(dn[vreg_rows], 1, 0))

    def edge_after(vreg_rows, token_row):
        return jnp.where(sub == SUBLANES - 1, dn_all[token_row:token_row + 1] * keep_next,
                         pltpu.roll(dn[vreg_rows], SUBLANES - 1, 0))

    b1 = edge_before(slice(tm - 8, tm), tm + 7)
    b2 = edge_before(slice(tm - 16, tm - 8), tm + 6)
    a1 = edge_after(slice(0, 8), tm + 8)
    a2 = edge_after(slice(8, 16), tm + 9)
    cat = jnp.concatenate
    cw = convw_ref[...]
    y = (cat([b2, b1, dn[:tm - 16]], axis=0) * cw[0:1] + cat([b1, dn[:tm - 8]], axis=0) * cw[1:2]
         + dn * cw[2:3]
         + cat([dn[8:], a1], axis=0) * cw[3:4] + cat([dn[16:], a1, a2], axis=0) * cw[4:5])
    y = _silu(y)
    for h in range(DN_HEADS):
        qh = y[:, h * LANES:(h + 1) * LANES]
        kh = y[:, DN_WIDTH + h * LANES: DN_WIDTH + (h + 1) * LANES]
        vh = y[:, 2 * DN_WIDTH + h * LANES: 2 * DN_WIDTH + (h + 1) * LANES]
        qh = qh * lax.rsqrt(jnp.sum(qh * qh, axis=-1, keepdims=True) + NORM_EPS) * (DN_HEAD_DIM ** -0.5)
        kh = kh * lax.rsqrt(jnp.sum(kh * kh, axis=-1, keepdims=True) + NORM_EPS)
        for j, (val, ref) in enumerate(((qh, dq_ref), (kh, dk_ref), (vh, dv_ref))):
            row_scr[j] = val
            for s in range(SUBLANES):
                ref[0, h, s * run:(s + 1) * run, :] = row_scr[j, pl.ds(s, run, stride=SUBLANES), :].astype(BF16)
    for h in range(DN_HEADS):
        gate_ref[0, h] = _silu(g[:, h * LANES:(h + 1) * LANES]).astype(BF16)


def _rope_tables(length):
    inv_freq = 1.0 / (10000.0 ** (jnp.arange(ROPE_HALF, dtype=F32) * 2.0 / ATT_HEAD_DIM))
    ang = jnp.arange(length, dtype=F32)[:, None] * inv_freq[None, :]
    cos, sin = jnp.cos(ang), jnp.sin(ang)
    cos128 = jnp.tile(cos, (1, LANES // ROPE_HALF))
    sin128 = jnp.tile(jnp.concatenate([-sin, sin], axis=-1), (1, LANES // ATT_HEAD_DIM))
    return cos128, sin128


def _inproj_call(x, norm1, w_in, att_q_norm, att_k_norm, dn_conv_w, dn_a_log, dn_dt_bias, *, tm=512):
    bsz, length, _ = x.shape
    nt = length // tm
    hb = tm // HALO_ROWS
    nhb = length // HALO_ROWS
    o3 = 3 * ATT_WIDTH
    o4 = o3 + 3 * DN_WIDTH
    o5 = o4 + DN_WIDTH
    wa = w_in[:, :o3].astype(BF16)
    wd = w_in[:, o3:o4].astype(BF16)
    wg = w_in[:, o4:o5].astype(BF16)
    wba = jnp.pad(w_in[:, o5:], ((0, 0), (0, LANES - 4 * DN_HEADS))).astype(BF16)
    qkg = jnp.stack([jnp.tile(att_q_norm, 2), jnp.tile(att_k_norm, 2)]
                    + [jnp.zeros((LANES,), F32)] * 6)
    convw = jnp.concatenate([dn_conv_w, jnp.zeros((3, 3 * DN_WIDTH), F32)], axis=0)
    pad_lanes = lambda v: jnp.pad(v.reshape(-1), (2 * DN_HEADS, LANES - 4 * DN_HEADS))
    gpar = jnp.stack([pad_lanes(dn_dt_bias), pad_lanes(dn_a_log)] + [jnp.zeros((LANES,), F32)] * 6)
    cos128, sin128 = _rope_tables(length)

    prev = lambda b, i: (b, jnp.maximum(i * hb - 1, 0), 0)
    nxt = lambda b, i: (b, jnp.minimum((i + 1) * hb, nhb - 1), 0)
    slab = jax.ShapeDtypeStruct((bsz, ATT_SLABS, length, LANES), BF16)
    slab_spec = pl.BlockSpec((1, ATT_SLABS, tm, LANES), lambda b, i: (b, 0, i, 0))
    out_shape, out_specs = [], []
    for _ in range(3):
        for d in DILATIONS:
            if d == 1:
                out_shape.append(slab)
                out_specs.append(slab_spec)
            else:
                out_shape.append(jax.ShapeDtypeStruct((bsz, d, ATT_SLABS, length // d, LANES), BF16))
                out_specs.append(pl.BlockSpec((1, d, ATT_SLABS, tm // d, LANES),
                                              lambda b, i: (b, 0, 0, i, 0)))
    out_shape += [slab] * 4
    out_specs += [slab_spec] * 4
    out_shape += [jax.ShapeDtypeStruct((bsz, length, LANES), F32),
                  jax.ShapeDtypeStruct((bsz, 16, length), F32)]
    out_specs += [pl.BlockSpec((1, tm, LANES), lambda b, i: (b, i, 0)),
                  pl.BlockSpec((1, 16, tm), lambda b, i: (b, 0, i))]

    return pl.pallas_call(
        functools.partial(_inproj_kernel, tm=tm),
        grid=(bsz, nt),
        in_specs=[
            pl.BlockSpec((1, tm, D_MODEL), lambda b, i: (b, i, 0)),
            pl.BlockSpec((1, HALO_ROWS, D_MODEL), prev),
            pl.BlockSpec((1, HALO_ROWS, D_MODEL), nxt),
            _const_spec((1, D_MODEL)),
            _const_spec((D_MODEL, o3)),
            _const_spec((D_MODEL, 3 * DN_WIDTH)),
            _const_spec((D_MODEL, DN_WIDTH)),
            _const_spec((D_MODEL, LANES)),
            _const_spec((8, LANES)),
            pl.BlockSpec((tm, LANES), lambda b, i: (i, 0)),
            pl.BlockSpec((tm, LANES), lambda b, i: (i, 0)),
            _const_spec((8, 3 * DN_WIDTH)),
            _const_spec((8, LANES)),
        ],
        out_specs=out_specs,
        out_shape=out_shape,
        scratch_shapes=[pltpu.VMEM((D_MODEL // LANES, tm, LANES), F32),
                        pltpu.VMEM((3 * ATT_SLABS, tm, LANES), F32),
                        pltpu.VMEM((4, tm // 4, LANES), F32)],
        compiler_params=pltpu.CompilerParams(
            dimension_semantics=("parallel", "arbitrary"), vmem_limit_bytes=VMEM_LIMIT),
        name="inproj",
    )(x, x, x, norm1.reshape(1, D_MODEL), wa, wd, wg, wba, qkg, cos128, sin128, convw, gpar)


ATT_TILE = 1024
ATT_QB = 64
ATT_RADIUS = 64
ATT_KW = ATT_QB + 2 * ATT_RADIUS
ATT_GROUP = 4
NEG_INF = -1e30


def _attn_kernel(q1, k1, k1p, k1n, v1, v1p, v1n,
                 q4, k4, k4p, k4n, v4, v4p, v4n,
                 q16, k16, k16p, k16n, v16, v16p, v16n,
                 o_ref, acc_o, acc_m, bias_scr):
    i = pl.program_id(1)
    last = pl.num_programs(1) - 1
    lane = lax.broadcasted_iota(jnp.int32, (1, LANES), 1)
    is_a = lane < ATT_HEAD_DIM

    qi = lax.broadcasted_iota(jnp.int32, (ATT_QB, ATT_KW), 0)
    kj = lax.broadcasted_iota(jnp.int32, (ATT_QB, ATT_KW), 1)
    band = jnp.where((kj >= qi) & (kj <= qi + 2 * ATT_RADIUS), 0.0, NEG_INF)
    first_seq = (i == 0).astype(F32)
    last_seq = (i == last).astype(F32)
    before = jnp.where(kj < ATT_RADIUS, NEG_INF, 0.0)
    after = jnp.where(kj >= ATT_QB + ATT_RADIUS, NEG_INF, 0.0)
    bias_scr[0] = band
    bias_scr[1] = band + first_seq * before
    bias_scr[2] = band + last_seq * after
    bias_scr[3] = band + first_seq * before + last_seq * after

    def unit_group(stage, blocks):
        items = [(bi, s) for bi in range(len(blocks)) for s in range(ATT_SLABS)]
        heads = [(bi, s, hh) for bi, s in items for hh in range(2)]
        s2, vext = {}, {}
        for bi, s in items:
            _, q_get, k_get, v_get, _, _ = blocks[bi]
            qt = q_get(s)
            vw = v_get(s)
            zero = jnp.zeros_like(qt)
            qq = jnp.concatenate([jnp.where(is_a, qt, zero), jnp.where(is_a, zero, qt)], axis=0)
            s2[bi, s] = lax.dot_general(qq, k_get(s), (((1,), (1,)), ((), ())),
                                        preferred_element_type=F32)
            one = jnp.ones_like(vw)
            vext[bi, s] = (jnp.where(is_a, vw, one), jnp.where(is_a, one, vw))
        m_prev, m_new, p16 = {}, {}, {}
        for it in heads:
            bi, s, hh = it
            rows = blocks[bi][4]
            sc = s2[bi, s][hh * ATT_QB:(hh + 1) * ATT_QB] + bias_scr[blocks[bi][0]]
            m_cur = jnp.max(sc, axis=-1, keepdims=True)
            if stage == 0:
                m_new[it] = jnp.broadcast_to(m_cur, (ATT_QB, LANES))
            else:
                m_prev[it] = acc_m[2 * s + hh, rows, :]
                m_new[it] = jnp.maximum(m_prev[it], m_cur)
            p16[it] = jnp.concatenate(
                [jnp.exp2(sc[:, :LANES] - m_new[it]),
                 jnp.exp2(sc[:, LANES:] - m_new[it][:, :ATT_KW - LANES])], axis=-1).astype(BF16)
        o = {it: jnp.dot(p16[it], vext[it[0], it[1]][it[2]], preferred_element_type=F32)
             for it in heads}
        for it in heads:
            bi, s, hh = it
            rows = blocks[bi][4]
            if stage > 0:
                o[it] = o[it] + jnp.exp2(m_prev[it] - m_new[it]) * acc_o[2 * s + hh, rows, :]
            if stage < 2:
                acc_m[2 * s + hh, rows, :] = m_new[it]
                acc_o[2 * s + hh, rows, :] = o[it]
        if stage == 2:
            for bi, s in items:
                norm = [o[bi, s, hh] * (1.0 / pltpu.roll(o[bi, s, hh], ATT_HEAD_DIM, 1)) for hh in range(2)]
                o_ref[0, s, blocks[bi][5], :] = jnp.where(is_a, norm[0], norm[1]).astype(BF16)

    def run_pattern(stage, d, q, k, kp, kn, v, vp, vn):
        ld = ATT_TILE // d
        nqb = ld // ATT_QB

        def at(ref, r, s, rows):
            return ref[0, s, rows, :] if d == 1 else ref[0, r, s, rows, :]

        def full(ref, r, s):
            return ref[0, s] if d == 1 else ref[0, r, s]

        def acc_rows(r, qb):
            start = qb * (ATT_QB * d) + r
            return pl.ds(start, ATT_QB) if d == 1 else pl.ds(start, ATT_QB, stride=d)

        def block(r, qb):
            q_rows = pl.ds(qb * ATT_QB, ATT_QB) if isinstance(qb, int) else \
                pl.ds(pl.multiple_of(qb * ATT_QB, ATT_QB), ATT_QB)
            q_get = lambda s: at(q, r, s, q_rows)
            if nqb == 1:
                pieces = lambda m, mp, mn: (lambda s: jnp.concatenate(
                    [full(mp, r, s), full(m, r, s), full(mn, r, s)], axis=0))
                return (3, q_get, pieces(k, kp, kn), pieces(v, vp, vn), acc_rows(r, 0), q_rows)
            if isinstance(qb, int) and qb == 0:
                head_rows = pl.ds(0, ATT_QB + ATT_RADIUS)
                pieces = lambda m, mp: (lambda s: jnp.concatenate(
                    [full(mp, r, s), at(m, r, s, head_rows)], axis=0))
                return (1, q_get, pieces(k, kp), pieces(v, vp), acc_rows(r, 0), q_rows)
            if isinstance(qb, int) and qb == nqb - 1:
                tail_rows = pl.ds(ld - ATT_QB - ATT_RADIUS, ATT_QB + ATT_RADIUS)
                pieces = lambda m, mn: (lambda s: jnp.concatenate(
                    [at(m, r, s, tail_rows), full(mn, r, s)], axis=0))
                return (2, q_get, pieces(k, kn), pieces(v, vn), acc_rows(r, qb), q_rows)
            start = qb * ATT_QB - ATT_RADIUS
            win = pl.ds(start if isinstance(qb, int) else pl.multiple_of(start, ATT_RADIUS), ATT_KW)
            return (0, q_get, lambda s: at(k, r, s, win), lambda s: at(v, r, s, win),
                    acc_rows(r, qb), q_rows)

        grp = ATT_GROUP
        if nqb == 1:
            def residues(j, carry):
                unit_group(stage, [block(grp * j + t, 0) for t in range(grp)])
                return carry
            lax.fori_loop(0, d // grp, residues, 0)
            return

        def residue(r, carry):
            if nqb == grp:
                unit_group(stage, [block(r, t) for t in range(nqb)])
                return carry
            unit_group(stage, [block(r, t) for t in range(grp)])

            def interior(j, c):
                unit_group(stage, [block(r, grp * j + t) for t in range(grp)])
                return c

            if nqb > 2 * grp:
                lax.fori_loop(1, nqb // grp - 1, interior, 0)
            unit_group(stage, [block(r, nqb - grp + t) for t in range(grp)])
            return carry

        if d == 1:
            residue(0, 0)
        else:
            lax.fori_loop(0, d, residue, 0)

    run_pattern(0, 16, q16, k16, k16p, k16n, v16, v16p, v16n)
    run_pattern(1, 4, q4, k4, k4p, k4n, v4, v4p, v4n)
    run_pattern(2, 1, q1, k1, k1p, k1n, v1, v1p, v1n)


def _attn_call(qkv):
    q1, q4, q16, k1, k4, k16, v1, v4, v16 = qkv
    bsz, _, length, _ = q1.shape
    nt = length // ATT_TILE
    args, specs = [], []
    for d, q, k, v in ((1, q1, k1, v1), (4, q4, k4, v4), (16, q16, k16, v16)):
        ld = ATT_TILE // d
        hb = ld // ATT_RADIUS
        nhb = length // d // ATT_RADIUS
        if d == 1:
            main = pl.BlockSpec((1, ATT_SLABS, ld, LANES), lambda b, i: (b, 0, i, 0))
            prev = pl.BlockSpec((1, ATT_SLABS, ATT_RADIUS, LANES),
                                lambda b, i, hb=hb: (b, 0, jnp.maximum(i * hb - 1, 0), 0))
            nxt = pl.BlockSpec((1, ATT_SLABS, ATT_RADIUS, LANES),
                               lambda b, i, hb=hb, nhb=nhb: (b, 0, jnp.minimum((i + 1) * hb, nhb - 1), 0))
        else:
            main = pl.BlockSpec((1, d, ATT_SLABS, ld, LANES), lambda b, i: (b, 0, 0, i, 0))
            prev = pl.BlockSpec((1, d, ATT_SLABS, ATT_RADIUS, LANES),
                                lambda b, i, hb=hb: (b, 0, 0, jnp.maximum(i * hb - 1, 0), 0))
            nxt = pl.BlockSpec((1, d, ATT_SLABS, ATT_RADIUS, LANES),
                               lambda b, i, hb=hb, nhb=nhb: (b, 0, 0, jnp.minimum((i + 1) * hb, nhb - 1), 0))
        args += [q, k, k, k, v, v, v]
        specs += [main, main, prev, nxt, main, prev, nxt]
    return pl.pallas_call(
        _attn_kernel,
        grid=(bsz, nt),
        in_specs=specs,
        out_specs=pl.BlockSpec((1, ATT_SLABS, ATT_TILE, LANES), lambda b, i: (b, 0, i, 0)),
        out_shape=jax.ShapeDtypeStruct((bsz, ATT_SLABS, length, LANES), BF16),
        scratch_shapes=[pltpu.VMEM((ATT_HEADS, ATT_TILE, LANES), F32),
                        pltpu.VMEM((ATT_HEADS, ATT_TILE, LANES), F32),
                        pltpu.VMEM((4, ATT_QB, ATT_KW), F32)],
        compiler_params=pltpu.CompilerParams(
            dimension_semantics=("parallel", "arbitrary"), vmem_limit_bytes=VMEM_LIMIT),
        name="dilated_attention",
    )(*args)


DN_TILE = 256
DN_ROWS = 2
DN_GROUP = 4


def _dot_nt(a, b, **kw):
    return lax.dot_general(a, b, (((1,), (1,)), ((), ())), preferred_element_type=F32, **kw)


def _deltanet_kernel(*refs, reverse, tile, rows_per_step):
    if reverse:
        dq_ref, dk_ref, dv_ref, col_ref, row_ref, o_ref, state = refs
    else:
        dq_ref, dk_ref, dv_ref, col_ref, row_ref, gate_ref, obwd_ref, gain_ref, o_ref, state = refs
    c_sz = DN_CHUNK
    nchunks = tile // c_sz

    @pl.when(pl.program_id(1) == 0)
    def _():
        state[...] = jnp.zeros_like(state)

    ii = lax.broadcasted_iota(jnp.int32, (c_sz, 2 * c_sz), 0)
    jj = lax.broadcasted_iota(jnp.int32, (c_sz, 2 * c_sz), 1) % c_sz
    incl = (jj >= ii) if reverse else (jj <= ii)
    strict = (jj > ii) if reverse else (jj < ii)
    edge = 0 if reverse else c_sz - 1
    lo = lax.broadcasted_iota(jnp.int32, (1, 2 * c_sz), 1) < c_sz

    order = list(range(nchunks - 1, -1, -1)) if reverse else list(range(nchunks))
    rows_of = lambda c: slice(c * c_sz, (c + 1) * c_sz)
    bf = lambda t: t.astype(BF16)
    dot = functools.partial(jnp.dot, preferred_element_type=F32)
    cat = jnp.concatenate

    def blockdiag(t16):
        zero = jnp.zeros_like(t16)
        return cat([jnp.where(lo, t16, zero), jnp.where(lo, zero, t16)], axis=0)

    pre = {}
    for g0 in range(0, nchunks, DN_GROUP):
        units = [(c, b, hp) for c in order[g0:g0 + DN_GROUP] for b in range(rows_per_step)
                 for hp in range(DN_HEADS // 2)]
        colt = {(c, b): col_ref[b, rows_of(c), :] for c, b, _ in units}
        rowt = {(c, b): row_ref[b, :, rows_of(c)] for c, b, _ in units}
        v = {}
        for un in units:
            c, b, hp = un
            per_head = []
            for h in (2 * hp, 2 * hp + 1):
                idx = (DN_HEADS if reverse else 0) + h
                ct = colt[c, b]
                beta = ct[:, idx:idx + 1]
                gcc = ct[:, 2 * DN_HEADS + idx:2 * DN_HEADS + idx + 1]
                gcr = rowt[c, b][2 * DN_HEADS + idx:2 * DN_HEADS + idx + 1, :]
                gl = gcc[edge:edge + 1, :]
                k16 = dk_ref[b, h, rows_of(c), :]
                q16 = dq_ref[b, h, rows_of(c), :]
                kf = k16.astype(F32)
                kbeta = kf * beta
                egc = jnp.exp(gcc)
                vf = dv_ref[b, h, rows_of(c), :].astype(F32)
                per_head.append(dict(
                    gcc=gcc, gcr=gcr, k16=k16, q16=q16, kb16=bf(kbeta), eg=jnp.exp(gl),
                    rhs=bf(cat([vf * beta, kbeta * egc], axis=-1)), rhs32=(vf * beta, kbeta * egc),
                    qd=bf(q16.astype(F32) * egc), kd=bf(kf * jnp.exp(gl - gcc))))
            a, bb = per_head
            zero = jnp.zeros_like(a["k16"])
            k_diag = cat([cat([a["k16"], zero], axis=1), cat([zero, bb["k16"]], axis=1)], axis=0)
            gdiff = (jnp.where(lo, a["gcc"], bb["gcc"]) - cat([a["gcr"], bb["gcr"]], axis=1))
            decay = jnp.exp(jnp.where(incl, gdiff, NEG_INF))
            v[un] = dict(heads=per_head, decay=decay,
                         kk=_dot_nt(cat([a["kb16"], bb["kb16"]], axis=1), k_diag),
                         qk=_dot_nt(cat([a["q16"], bb["q16"]], axis=1), k_diag))
        mpow = {un: -jnp.where(strict, v[un]["kk"] * v[un]["decay"], 0.0) for un in units}
        tlow = dict(mpow)
        for _ in range(5):
            m16 = {un: bf(mpow[un]) for un in units}
            mpow = {un: dot(m16[un], blockdiag(m16[un])) for un in units}
            tlow = {un: tlow[un] + mpow[un] + dot(bf(tlow[un]), blockdiag(bf(mpow[un]))) for un in units}
        for un in units:
            c, b, hp = un
            a, bb = v[un]["heads"]
            t16 = bf(tlow[un])
            zero = jnp.zeros_like(t16)
            rhs2 = cat([a["rhs"], bb["rhs"]], axis=0)
            qk16 = bf(v[un]["qk"] * v[un]["decay"])
            for hh, hd in enumerate((a, bb)):
                mine = lo if hh == 0 else jnp.logical_not(lo)
                sol = dot(jnp.where(mine, t16, zero), rhs2)
                pre[c, b, 2 * hp + hh] = dict(
                    u=hd["rhs32"][0] + sol[:, :DN_HEAD_DIM],
                    wq=cat([bf(hd["rhs32"][1] + sol[:, DN_HEAD_DIM:]), hd["qd"]], axis=0),
                    qk=jnp.where(mine, qk16, jnp.zeros_like(qk16)), kd=hd["kd"], eg=hd["eg"])

    chains = [(b, h) for b in range(rows_per_step) for h in range(DN_HEADS)]
    for c in order:
        st = {n: state[n[0] * DN_HEADS + n[1]] for n in chains}
        st16 = {n: bf(st[n]) for n in chains}
        ws = {n: dot(pre[(c,) + n]["wq"], st16[n]) for n in chains}
        v16 = {n: bf(pre[(c,) + n]["u"] - ws[n][:c_sz]) for n in chains}
        vpair = {(b, h // 2): cat([v16[b, h - h % 2], v16[b, h - h % 2 + 1]], axis=0) for b, h in chains}
        ov = {n: dot(pre[(c,) + n]["qk"], vpair[n[0], n[1] // 2]) for n in chains}
        sv = {n: lax.dot_general(pre[(c,) + n]["kd"], v16[n], (((0,), (0,)), ((), ())),
                                 preferred_element_type=F32) for n in chains}
        for n in chains:
            b, h = n
            state[b * DN_HEADS + h] = st[n] * pre[(c,) + n]["eg"] + sv[n]
            o = ws[n][c_sz:] + ov[n]
            if reverse:
                o_ref[b, h, rows_of(c), :] = o
            else:
                tot = o + obwd_ref[b, h, rows_of(c), :]
                y = tot * _rms_scale(tot) * gain_ref[...] * gate_ref[b, h, rows_of(c), :].astype(F32)
                o_ref[b, h, rows_of(c), :] = y.astype(BF16)


def _deltanet_call(dq, dk, dv, col, row, gate, dn_out_norm, *, tile=DN_TILE, rows_per_step=DN_ROWS):
    bsz, _, length, _ = dq.shape
    assert bsz % rows_per_step == 0 and length % tile == 0
    nt = length // tile
    nb = bsz // rows_per_step

    def run(reverse, extra_args, extra_specs, out_dtype):
        pos = (lambda i: nt - 1 - i) if reverse else (lambda i: i)
        slab = pl.BlockSpec((rows_per_step, DN_HEADS, tile, LANES), lambda b, i: (b, 0, pos(i), 0))
        return pl.pallas_call(
            functools.partial(_deltanet_kernel, reverse=reverse, tile=tile, rows_per_step=rows_per_step),
            grid=(nb, nt),
            in_specs=[slab, slab, slab,
                      pl.BlockSpec((rows_per_step, tile, LANES), lambda b, i: (b, pos(i), 0)),
                      pl.BlockSpec((rows_per_step, 16, tile), lambda b, i: (b, 0, pos(i)))] + extra_specs(slab),
            out_specs=slab,
            out_shape=jax.ShapeDtypeStruct((bsz, DN_HEADS, length, LANES), out_dtype),
            scratch_shapes=[pltpu.VMEM((rows_per_step * DN_HEADS, DN_HEAD_DIM, DN_HEAD_DIM), F32)],
            compiler_params=pltpu.CompilerParams(
                dimension_semantics=("parallel", "arbitrary"), vmem_limit_bytes=VMEM_LIMIT),
            name="deltanet_bwd" if reverse else "deltanet_fwd",
        )(dq, dk, dv, col, row, *extra_args)

    o_bwd = run(True, (), lambda slab: [], F32)
    return run(False, (gate, o_bwd, dn_out_norm.reshape(1, DN_HEAD_DIM)),
               lambda slab: [slab, slab, pl.BlockSpec((1, DN_HEAD_DIM), lambda b, i: (0, 0))], BF16)


def _ffn_kernel(x_ref, xp_ref, xn_ref, att_ref, attp_ref, attn_ref, dn_ref, dnp_ref, dnn_ref,
                wout_ref, norm2_ref, wg_ref, wu_ref, cg_ref, cu_ref, wd_ref, o_ref, perm_scr, *, tm):
    i = pl.program_id(1)
    last = pl.num_programs(1) - 1
    gain = norm2_ref[...]
    run = tm // SUBLANES

    half = N_SLABS // 2
    xh = jnp.concatenate([xp_ref[0, HALO_ROWS - 8:, :], xn_ref[0, :8, :]], axis=0)
    mh = ([jnp.concatenate([attp_ref[0, s], attn_ref[0, s]], axis=0) for s in range(half)]
          + [jnp.concatenate([dnp_ref[0, s], dnn_ref[0, s]], axis=0) for s in range(half)])
    mh = [jnp.concatenate([m[HALO_ROWS - 8:HALO_ROWS], m[HALO_ROWS:HALO_ROWS + 8]], axis=0) for m in mh]
    main = [att_ref[0, s] for s in range(half)] + [dn_ref[0, s] for s in range(half)]
    mix = jnp.concatenate([jnp.concatenate([a, b], axis=0) for a, b in zip(main, mh)], axis=-1)
    h_all = (jnp.concatenate([x_ref[0], xh], axis=0)
             + jnp.dot(mix, wout_ref[...], preferred_element_type=F32))
    n_all = h_all * _rms_scale(h_all) * gain
    h = h_all[:tm]

    for s in range(SUBLANES):
        for sl in range(N_SLABS):
            perm_scr[sl, pl.ds(s, run, stride=SUBLANES), :] = n_all[s * run:(s + 1) * run,
                                                                    sl * LANES:(sl + 1) * LANES]
    n2_all = jnp.concatenate([jnp.concatenate([perm_scr[sl] for sl in range(N_SLABS)], axis=-1),
                              n_all[tm:]], axis=0).astype(BF16)
    sub = lax.broadcasted_iota(jnp.int32, (SUBLANES, 1), 0)
    keep_prev = (i > 0).astype(F32)
    keep_next = (i < last).astype(F32)

    def up_proj(c):
        return [jnp.dot(n2_all, w_ref[c], preferred_element_type=F32) for w_ref in (wg_ref, wu_ref)]

    def conv(c, y_all, cw_ref):
        y = y_all[:tm]
        before = y_all[tm + 7:tm + 8] * keep_prev
        after = y_all[tm + 8:tm + 9] * keep_next
        y_prev0 = jnp.where(sub == 0, before, pltpu.roll(y[tm - SUBLANES:], 1, 0))
        y_next_last = jnp.where(sub == SUBLANES - 1, after, pltpu.roll(y[:SUBLANES], SUBLANES - 1, 0))
        y_prev = jnp.concatenate([y_prev0, y[:tm - SUBLANES]], axis=0)
        y_next = jnp.concatenate([y[SUBLANES:], y_next_last], axis=0)
        cw = cw_ref[c]
        return y_prev * cw[0:1] + y * cw[1:2] + y_next * cw[2:3] + cw[3:4]

    acc = jnp.zeros((tm, D_MODEL), F32)
    ahead = up_proj(0)
    for c in range(N_FFN_CHUNKS):
        yg, yu = ahead
        if c + 1 < N_FFN_CHUNKS:
            ahead = up_proj(c + 1)
        act = (_silu(conv(c, yg, cg_ref)) * conv(c, yu, cu_ref)).astype(BF16)
        acc = acc + jnp.dot(act, wd_ref[c], preferred_element_type=F32)
    for sl in range(N_SLABS):
        perm_scr[sl] = acc[:, sl * LANES:(sl + 1) * LANES]
    for s in range(SUBLANES):
        for sl in range(N_SLABS):
            o_ref[0, s * run:(s + 1) * run, sl * LANES:(sl + 1) * LANES] = (
                h[s * run:(s + 1) * run, sl * LANES:(sl + 1) * LANES]
                + perm_scr[sl, pl.ds(s, run, stride=SUBLANES), :])


def _ffn_call(x, att, dn, w_out, norm2, w_up, ffn_conv_w, ffn_conv_b, w_down, *, tm=256):
    bsz, length, _ = x.shape
    nt = length // tm
    hb = tm // HALO_ROWS
    nhb = length // HALO_ROWS

    def chunked_cols(w):
        return w.reshape(w.shape[0], N_FFN_CHUNKS, FFN_CHUNK).transpose(1, 0, 2)

    wg = chunked_cols(w_up[:, :FFN_DIM]).astype(BF16)
    wu = chunked_cols(w_up[:, FFN_DIM:]).astype(BF16)
    wd = w_down.reshape(N_FFN_CHUNKS, FFN_CHUNK, D_MODEL).astype(BF16)

    def conv_table(lo):
        t = jnp.concatenate([ffn_conv_w[:, lo:lo + FFN_DIM], ffn_conv_b[None, lo:lo + FFN_DIM],
                             jnp.zeros((4, FFN_DIM), F32)], axis=0)
        return chunked_cols(t)

    cg, cu = conv_table(0), conv_table(FFN_DIM)

    prev = lambda b, i: (b, jnp.maximum(i * hb - 1, 0), 0)
    nxt = lambda b, i: (b, jnp.minimum((i + 1) * hb, nhb - 1), 0)
    prev4 = lambda b, i: (b, 0, jnp.maximum(i * hb - 1, 0), 0)
    nxt4 = lambda b, i: (b, 0, jnp.minimum((i + 1) * hb, nhb - 1), 0)

    return pl.pallas_call(
        functools.partial(_ffn_kernel, tm=tm),
        grid=(bsz, nt),
        in_specs=[
            pl.BlockSpec((1, tm, D_MODEL), lambda b, i: (b, i, 0)),
            pl.BlockSpec((1, HALO_ROWS, D_MODEL), prev),
            pl.BlockSpec((1, HALO_ROWS, D_MODEL), nxt),
            pl.BlockSpec((1, N_SLABS // 2, tm, LANES), lambda b, i: (b, 0, i, 0)),
            pl.BlockSpec((1, N_SLABS // 2, HALO_ROWS, LANES), prev4),
            pl.BlockSpec((1, N_SLABS // 2, HALO_ROWS, LANES), nxt4),
            pl.BlockSpec((1, N_SLABS // 2, tm, LANES), lambda b, i: (b, 0, i, 0)),
            pl.BlockSpec((1, N_SLABS // 2, HALO_ROWS, LANES), prev4),
            pl.BlockSpec((1, N_SLABS // 2, HALO_ROWS, LANES), nxt4),
            _const_spec((D_MODEL, D_MODEL)),
            _const_spec((1, D_MODEL)),
            _const_spec((N_FFN_CHUNKS, D_MODEL, FFN_CHUNK)),
            _const_spec((N_FFN_CHUNKS, D_MODEL, FFN_CHUNK)),
            _const_spec((N_FFN_CHUNKS, 8, FFN_CHUNK)),
            _const_spec((N_FFN_CHUNKS, 8, FFN_CHUNK)),
            _const_spec((N_FFN_CHUNKS, FFN_CHUNK, D_MODEL)),
        ],
        out_specs=pl.BlockSpec((1, tm, D_MODEL), lambda b, i: (b, i, 0)),
        out_shape=jax.ShapeDtypeStruct((bsz, length, D_MODEL), F32),
        scratch_shapes=[pltpu.VMEM((N_SLABS, tm, LANES), F32)],
        compiler_params=pltpu.CompilerParams(
            dimension_semantics=("parallel", "arbitrary"), vmem_limit_bytes=VMEM_LIMIT),
        name="outproj_convglu",
    )(x, x, x, att, att, att, dn, dn, dn, w_out.astype(BF16), norm2.reshape(1, D_MODEL),
      wg, wu, cg, cu, wd)


def _layer(x, norm1, w_in, att_q_norm, att_k_norm, dn_conv_w, dn_a_log, dn_dt_bias, dn_out_norm,
           w_out, norm2, w_up, ffn_conv_w, ffn_conv_b, w_down):
    outs = _inproj_call(x, norm1, w_in, att_q_norm, att_k_norm, dn_conv_w, dn_a_log, dn_dt_bias)
    att = _attn_call(outs[:9])
    dq, dk, dv, gate, col, row = outs[9:]
    dn = _deltanet_call(dq, dk, dv, col, row, gate, dn_out_norm)
    return _ffn_call(x, att, dn, w_out, norm2, w_up, ffn_conv_w, ffn_conv_b, w_down)


def kernel(x_prompt, x_sample, norm1, w_in, att_q_norm, att_k_norm, dn_conv_w, dn_a_log, dn_dt_bias,
           dn_out_norm, w_out, norm2, w_up, ffn_conv_w, ffn_conv_b, w_down):
    def trunk(x):
        for l in range(norm1.shape[0]):
            x = _layer(x, norm1[l], w_in[l], att_q_norm[l], att_k_norm[l], dn_conv_w[l], dn_a_log[l],
                       dn_dt_bias[l], dn_out_norm[l], w_out[l], norm2[l], w_up[l], ffn_conv_w[l],
                       ffn_conv_b[l], w_down[l])
        return x

    return trunk(x_prompt), trunk(x_sample)
```

```python
import functools
import math

import jax
import jax.numpy as jnp
from jax import lax
from jax.experimental import pallas as pl
from jax.experimental.pallas import tpu as pltpu

D_MODEL = 1024
ATT_HEADS = 8
ATT_HEAD_DIM = 64
ATT_WIDTH = ATT_HEADS * ATT_HEAD_DIM
DN_HEADS = 4
DN_HEAD_DIM = 128
DN_WIDTH = DN_HEADS * DN_HEAD_DIM
DN_CONV = 5
DN_CHUNK = 64
FFN_DIM = 2816
FFN_CONV = 3
NORM_EPS = 1e-6

LANES = 128
SUBLANES = 8
HALO_ROWS = 16
N_SLABS = (ATT_WIDTH + DN_WIDTH) // LANES
FFN_CHUNK = 256
N_FFN_CHUNKS = FFN_DIM // FFN_CHUNK
VMEM_LIMIT = 56 * 1024 * 1024

F32 = jnp.float32
BF16 = jnp.bfloat16


def _rms_scale(xf):
    return lax.rsqrt(jnp.mean(xf * xf, axis=-1, keepdims=True) + NORM_EPS)


def _silu(x):
    return x * (1.0 / (1.0 + jnp.exp(-x)))


def _const_spec(shape):
    nd = len(shape)
    return pl.BlockSpec(shape, lambda b, i: (0,) * nd, pipeline_mode=pl.Buffered(1))


ATT_SLABS = ATT_WIDTH // LANES
DILATIONS = (1, 4, 16)
ROPE_HALF = ATT_HEAD_DIM // 2
ATT_Q_SCALE = math.log2(math.e) / math.sqrt(ATT_HEAD_DIM)


def _inproj_kernel(x_ref, xp_ref, xn_ref, norm1_ref, wa_ref, wd_ref, wg_ref, wba_ref, qkg_ref,
                   cos_ref, sin_ref, convw_ref, gpar_ref,
                   q1_ref, q4_ref, q16_ref, k1_ref, k4_ref, k16_ref, v1_ref, v4_ref, v16_ref,
                   dq_ref, dk_ref, dv_ref, gate_ref, col_ref, row_ref,
                   row_scr, perm_scr, mod4_scr, *, tm):
    i = pl.program_id(1)
    last = pl.num_programs(1) - 1
    gain1 = norm1_ref[...]

    def normed(x):
        return x * _rms_scale(x) * gain1

    nf = normed(x_ref[0])
    n = nf.astype(BF16)
    nh = normed(jnp.concatenate([xp_ref[0, HALO_ROWS - 8:, :], xn_ref[0, :8, :]], axis=0)).astype(BF16)
    run = tm // SUBLANES
    for s in range(SUBLANES):
        for sl in range(D_MODEL // LANES):
            row_scr[sl, pl.ds(s, run, stride=SUBLANES), :] = nf[s * run:(s + 1) * run,
                                                               sl * LANES:(sl + 1) * LANES]
    n_rows = jnp.concatenate([row_scr[sl] for sl in range(D_MODEL // LANES)], axis=-1).astype(BF16)

    att = jnp.dot(n, wa_ref[...], preferred_element_type=F32)
    dn_all = jnp.dot(jnp.concatenate([n_rows, nh], axis=0), wd_ref[...], preferred_element_type=F32)
    g = jnp.dot(n, wg_ref[...], preferred_element_type=F32)
    ba = jnp.dot(n, wba_ref[...], preferred_element_type=F32)

    lane = lax.broadcasted_iota(jnp.int32, (1, LANES), 1)
    gp = gpar_ref[...]
    beta = 1.0 / (1.0 + jnp.exp(-ba))
    z = ba + gp[0:1]
    softplus = jnp.maximum(z, 0.0) + jnp.log(1.0 + jnp.exp(-jnp.abs(z)))
    colv = jnp.where(lane < 2 * DN_HEADS, beta, -jnp.exp(gp[1:2]) * softplus)
    t = colv.T
    gl = t[8:16]
    pos = lax.broadcasted_iota(jnp.int32, (1, tm), 1) % DN_CHUNK
    pre, suf = gl, gl
    step = 1
    while step < DN_CHUNK:
        pre = pre + jnp.where(pos >= step, pltpu.roll(pre, step, 1), 0.0)
        suf = suf + jnp.where(pos < DN_CHUNK - step, pltpu.roll(suf, tm - step, 1), 0.0)
        step *= 2
    sub = lax.broadcasted_iota(jnp.int32, (SUBLANES, 1), 0)
    gc = jnp.where(sub < DN_HEADS, pre, suf)
    row_ref[0] = jnp.concatenate([t[0:8], gc], axis=0)
    col_ref[0] = jnp.concatenate([t[0:8], gc, t[16:]], axis=0).T

    head_a = lane < ATT_HEAD_DIM
    first_half = (lane % ATT_HEAD_DIM) < ROPE_HALF
    cos = cos_ref[...]
    sin = sin_ref[...]
    for s in range(ATT_SLABS):
        for which in range(2):
            t = att[:, which * ATT_WIDTH + s * LANES: which * ATT_WIDTH + (s + 1) * LANES]
            ss = t * t
            sa = jnp.sum(jnp.where(head_a, ss, 0.0), axis=-1, keepdims=True)
            sb = jnp.sum(jnp.where(head_a, 0.0, ss), axis=-1, keepdims=True)
            ms = jnp.where(head_a, sa, sb) * (1.0 / ATT_HEAD_DIM)
            tn = t * lax.rsqrt(ms + NORM_EPS) * qkg_ref[which:which + 1, :]
            rot = jnp.where(first_half, pltpu.roll(tn, LANES - ROPE_HALF, 1), pltpu.roll(tn, ROPE_HALF, 1))
            r = tn * cos + rot * sin
            if which == 0:
                r = r * ATT_Q_SCALE
            perm_scr[which * ATT_SLABS + s] = r
        perm_scr[2 * ATT_SLABS + s] = att[:, 2 * ATT_WIDTH + s * LANES: 2 * ATT_WIDTH + (s + 1) * LANES]
    outs = ((q1_ref, q4_ref, q16_ref), (k1_ref, k4_ref, k16_ref), (v1_ref, v4_ref, v16_ref))
    for which in range(3):
        o1, o4, o16 = outs[which]
        for s in range(ATT_SLABS):
            slab = which * ATT_SLABS + s
            o1[0, s] = perm_scr[slab].astype(BF16)
            for r in range(4):
                quarter = perm_scr[slab, pl.ds(r, tm // 4, stride=4), :]
                o4[0, r, s] = quarter.astype(BF16)
                mod4_scr[r] = quarter
            for r in range(4):
                for r2 in range(4):
                    o16[0, r + 4 * r2, s] = mod4_scr[r, pl.ds(r2, tm // 16, stride=4), :].astype(BF16)

    keep_prev = (i > 0).astype(F32)
    keep_next = (i < last).astype(F32)
    sub = lax.broadcasted_iota(jnp.int32, (SUBLANES, 1), 0)
    dn = dn_all[:tm]

    def edge_before(vreg_rows, token_row):
        return jnp.where(sub == 0, dn_all[token_row:token_row + 1] * keep_prev,
                         pltpu.roll(dn[vreg_rows], 1, 0))

    def edge_after(vreg_rows, token_row):
        return jnp.where(sub == SUBLANES - 1, dn_all[token_row:token_row + 1] * keep_next,
                         pltpu.roll(dn[vreg_rows], SUBLANES - 1, 0))

    b1 = edge_before(slice(tm - 8, tm), tm + 7)
    b2 = edge_before(slice(tm - 16, tm - 8), tm + 6)
    a1 = edge_after(slice(0, 8), tm + 8)
    a2 = edge_after(slice(8, 16), tm + 9)
    cat = jnp.concatenate
    cw = convw_ref[...]
    y = (cat([b2, b1, dn[:tm - 16]], axis=0) * cw[0:1] + cat([b1, dn[:tm - 8]], axis=0) * cw[1:2]
         + dn * cw[2:3]
         + cat([dn[8:], a1], axis=0) * cw[3:4] + cat([dn[16:], a1, a2], axis=0) * cw[4:5])
    y = _silu(y)
    for h in range(DN_HEADS):
        qh = y[:, h * LANES:(h + 1) * LANES]
        kh = y[:, DN_WIDTH + h * LANES: DN_WIDTH + (h + 1) * LANES]
        vh = y[:, 2 * DN_WIDTH + h * LANES: 2 * DN_WIDTH + (h + 1) * LANES]
        qh = qh * lax.rsqrt(jnp.sum(qh * qh, axis=-1, keepdims=True) + NORM_EPS) * (DN_HEAD_DIM ** -0.5)
        kh = kh * lax.rsqrt(jnp.sum(kh * kh, axis=-1, keepdims=True) + NORM_EPS)
        for j, (val, ref) in enumerate(((qh, dq_ref), (kh, dk_ref), (vh, dv_ref))):
            row_scr[j] = val
            for s in range(SUBLANES):
                ref[0, h, s * run:(s + 1) * run, :] = row_scr[j, pl.ds(s, run, stride=SUBLANES), :].astype(BF16)
    for h in range(DN_HEADS):
        gate_ref[0, h] = _silu(g[:, h * LANES:(h + 1) * LANES]).astype(BF16)


def _rope_tables(length):
    inv_freq = 1.0 / (10000.0 ** (jnp.arange(ROPE_HALF, dtype=F32) * 2.0 / ATT_HEAD_DIM))
    ang = jnp.arange(length, dtype=F32)[:, None] * inv_freq[None, :]
    cos, sin = jnp.cos(ang), jnp.sin(ang)
    cos128 = jnp.tile(cos, (1, LANES // ROPE_HALF))
    sin128 = jnp.tile(jnp.concatenate([-sin, sin], axis=-1), (1, LANES // ATT_HEAD_DIM))
    return cos128, sin128


def _inproj_call(x, norm1, w_in, att_q_norm, att_k_norm, dn_conv_w, dn_a_log, dn_dt_bias, *, tm=512):
    bsz, length, _ = x.shape
    nt = length // tm
    hb = tm // HALO_ROWS
    nhb = length // HALO_ROWS
    o3 = 3 * ATT_WIDTH
    o4 = o3 + 3 * DN_WIDTH
    o5 = o4 + DN_WIDTH
    wa = w_in[:, :o3].astype(BF16)
    wd = w_in[:, o3:o4].astype(BF16)
    wg = w_in[:, o4:o5].astype(BF16)
    wba = jnp.pad(w_in[:, o5:], ((0, 0), (0, LANES - 4 * DN_HEADS))).astype(BF16)
    qkg = jnp.stack([jnp.tile(att_q_norm, 2), jnp.tile(att_k_norm, 2)]
                    + [jnp.zeros((LANES,), F32)] * 6)
    convw = jnp.concatenate([dn_conv_w, jnp.zeros((3, 3 * DN_WIDTH), F32)], axis=0)
    pad_lanes = lambda v: jnp.pad(v.reshape(-1), (2 * DN_HEADS, LANES - 4 * DN_HEADS))
    gpar = jnp.stack([pad_lanes(dn_dt_bias), pad_lanes(dn_a_log)] + [jnp.zeros((LANES,), F32)] * 6)
    cos128, sin128 = _rope_tables(length)

    prev = lambda b, i: (b, jnp.maximum(i * hb - 1, 0), 0)
    nxt = lambda b, i: (b, jnp.minimum((i + 1) * hb, nhb - 1), 0)
    slab = jax.ShapeDtypeStruct((bsz, ATT_SLABS, length, LANES), BF16)
    slab_spec = pl.BlockSpec((1, ATT_SLABS, tm, LANES), lambda b, i: (b, 0, i, 0))
    out_shape, out_specs = [], []
    for _ in range(3):
        for d in DILATIONS:
            if d == 1:
                out_shape.append(slab)
                out_specs.append(slab_spec)
            else:
                out_shape.append(jax.ShapeDtypeStruct((bsz, d, ATT_SLABS, length // d, LANES), BF16))
                out_specs.append(pl.BlockSpec((1, d, ATT_SLABS, tm // d, LANES),
                                              lambda b, i: (b, 0, 0, i, 0)))
    out_shape += [slab] * 4
    out_specs += [slab_spec] * 4
    out_shape += [jax.ShapeDtypeStruct((bsz, length, LANES), F32),
                  jax.ShapeDtypeStruct((bsz, 16, length), F32)]
    out_specs += [pl.BlockSpec((1, tm, LANES), lambda b, i: (b, i, 0)),
                  pl.BlockSpec((1, 16, tm), lambda b, i: (b, 0, i))]

    return pl.pallas_call(
        functools.partial(_inproj_kernel, tm=tm),
        grid=(bsz, nt),
        in_specs=[
            pl.BlockSpec((1, tm, D_MODEL), lambda b, i: (b, i, 0)),
            pl.BlockSpec((1, HALO_ROWS, D_MODEL), prev),
            pl.BlockSpec((1, HALO_ROWS, D_MODEL), nxt),
            _const_spec((1, D_MODEL)),
            _const_spec((D_MODEL, o3)),
            _const_spec((D_MODEL, 3 * DN_WIDTH)),
            _const_spec((D_MODEL, DN_WIDTH)),
            _const_spec((D_MODEL, LANES)),
            _const_spec((8, LANES)),
            pl.BlockSpec((tm, LANES), lambda b, i: (i, 0)),
            pl.BlockSpec((tm, LANES), lambda b, i: (i, 0)),
            _const_spec((8, 3 * DN_WIDTH)),
            _const_spec((8, LANES)),
        ],
        out_specs=out_specs,
        out_shape=out_shape,
        scratch_shapes=[pltpu.VMEM((D_MODEL // LANES, tm, LANES), F32),
                        pltpu.VMEM((3 * ATT_SLABS, tm, LANES), F32),
                        pltpu.VMEM((4, tm // 4, LANES), F32)],
        compiler_params=pltpu.CompilerParams(
            dimension_semantics=("parallel", "arbitrary"), vmem_limit_bytes=VMEM_LIMIT),
        name="inproj",
    )(x, x, x, norm1.reshape(1, D_MODEL), wa, wd, wg, wba, qkg, cos128, sin128, convw, gpar)


ATT_TILE = 1024
ATT_QB = 64
ATT_RADIUS = 64
ATT_KW = ATT_QB + 2 * ATT_RADIUS
ATT_GROUP = 4
NEG_INF = -1e30


def _attn_kernel(q1, k1, k1p, k1n, v1, v1p, v1n,
                 q4, k4, k4p, k4n, v4, v4p, v4n,
                 q16, k16, k16p, k16n, v16, v16p, v16n,
                 o_ref, acc_o, acc_m, bias_scr):
    i = pl.program_id(1)
    last = pl.num_programs(1) - 1
    lane = lax.broadcasted_iota(jnp.int32, (1, LANES), 1)
    is_a = lane < ATT_HEAD_DIM

    qi = lax.broadcasted_iota(jnp.int32, (ATT_QB, ATT_KW), 0)
    kj = lax.broadcasted_iota(jnp.int32, (ATT_QB, ATT_KW), 1)
    band = jnp.where((kj >= qi) & (kj <= qi + 2 * ATT_RADIUS), 0.0, NEG_INF)
    first_seq = (i == 0).astype(F32)
    last_seq = (i == last).astype(F32)
    before = jnp.where(kj < ATT_RADIUS, NEG_INF, 0.0)
    after = jnp.where(kj >= ATT_QB + ATT_RADIUS, NEG_INF, 0.0)
    bias_scr[0] = band
    bias_scr[1] = band + first_seq * before
    bias_scr[2] = band + last_seq * after
    bias_scr[3] = band + first_seq * before + last_seq * after

    def unit_group(stage, blocks):
        items = [(bi, s) for bi in range(len(blocks)) for s in range(ATT_SLABS)]
        heads = [(bi, s, hh) for bi, s in items for hh in range(2)]
        s2, vext = {}, {}
        for bi, s in items:
            _, q_get, k_get, v_get, _, _ = blocks[bi]
            qt = q_get(s)
            vw = v_get(s)
            zero = jnp.zeros_like(qt)
            qq = jnp.concatenate([jnp.where(is_a, qt, zero), jnp.where(is_a, zero, qt)], axis=0)
            s2[bi, s] = lax.dot_general(qq, k_get(s), (((1,), (1,)), ((), ())),
                                        preferred_element_type=F32)
            one = jnp.ones_like(vw)
            vext[bi, s] = (jnp.where(is_a, vw, one), jnp.where(is_a, one, vw))
        m_prev, m_new, p16 = {}, {}, {}
        for it in heads:
            bi, s, hh = it
            rows = blocks[bi][4]
            sc = s2[bi, s][hh * ATT_QB:(hh + 1) * ATT_QB] + bias_scr[blocks[bi][0]]
            m_cur = jnp.max(sc, axis=-1, keepdims=True)
            if stage == 0:
                m_new[it] = jnp.broadcast_to(m_cur, (ATT_QB, LANES))
            else:
                m_prev[it] = acc_m[2 * s + hh, rows, :]
                m_new[it] = jnp.maximum(m_prev[it], m_cur)
            p16[it] = jnp.concatenate(
                [jnp.exp2(sc[:, :LANES] - m_new[it]),
                 jnp.exp2(sc[:, LANES:] - m_new[it][:, :ATT_KW - LANES])], axis=-1).astype(BF16)
        o = {it: jnp.dot(p16[it], vext[it[0], it[1]][it[2]], preferred_element_type=F32)
             for it in heads}
        for it in heads:
            bi, s, hh = it
            rows = blocks[bi][4]
            if stage > 0:
                o[it] = o[it] + jnp.exp2(m_prev[it] - m_new[it]) * acc_o[2 * s + hh, rows, :]
            if stage < 2:
                acc_m[2 * s + hh, rows, :] = m_new[it]
                acc_o[2 * s + hh, rows, :] = o[it]
        if stage == 2:
            for bi, s in items:
                norm = [o[bi, s, hh] * (1.0 / pltpu.roll(o[bi, s, hh], ATT_HEAD_DIM, 1)) for hh in range(2)]
                o_ref[0, s, blocks[bi][5], :] = jnp.where(is_a, norm[0], norm[1]).astype(BF16)

    def run_pattern(stage, d, q, k, kp, kn, v, vp, vn):
        ld = ATT_TILE // d
        nqb = ld // ATT_QB

        def at(ref, r, s, rows):
            return ref[0, s, rows, :] if d == 1 else ref[0, r, s, rows, :]

        def full(ref, r, s):
            return ref[0, s] if d == 1 else ref[0, r, s]

        def acc_rows(r, qb):
            start = qb * (ATT_QB * d) + r
            return pl.ds(start, ATT_QB) if d == 1 else pl.ds(start, ATT_QB, stride=d)

        def block(r, qb):
            q_rows = pl.ds(qb * ATT_QB, ATT_QB) if isinstance(qb, int) else \
                pl.ds(pl.multiple_of(qb * ATT_QB, ATT_QB), ATT_QB)
            q_get = lambda s: at(q, r, s, q_rows)
            if nqb == 1:
                pieces = lambda m, mp, mn: (lambda s: jnp.concatenate(
                    [full(mp, r, s), full(m, r, s), full(mn, r, s)], axis=0))
                return (3, q_get, pieces(k, kp, kn), pieces(v, vp, vn), acc_rows(r, 0), q_rows)
            if isinstance(qb, int) and qb == 0:
                head_rows = pl.ds(0, ATT_QB + ATT_RADIUS)
                pieces = lambda m, mp: (lambda s: jnp.concatenate(
                    [full(mp, r, s), at(m, r, s, head_rows)], axis=0))
                return (1, q_get, pieces(k, kp), pieces(v, vp), acc_rows(r, 0), q_rows)
            if isinstance(qb, int) and qb == nqb - 1:
                tail_rows = pl.ds(ld - ATT_QB - ATT_RADIUS, ATT_QB + ATT_RADIUS)
                pieces = lambda m, mn: (lambda s: jnp.concatenate(
                    [at(m, r, s, tail_rows), full(mn, r, s)], axis=0))
                return (2, q_get, pieces(k, kn), pieces(v, vn), acc_rows(r, qb), q_rows)
            start = qb * ATT_QB - ATT_RADIUS
            win = pl.ds(start if isinstance(qb, int) else pl.multiple_of(start, ATT_RADIUS), ATT_KW)
            return (0, q_get, lambda s: at(k, r, s, win), lambda s: at(v, r, s, win),
                    acc_rows(r, qb), q_rows)

        grp = ATT_GROUP
        if nqb == 1:
            def residues(j, carry):
                unit_group(stage, [block(grp * j + t, 0) for t in range(grp)])
                return carry
            lax.fori_loop(0, d // grp, residues, 0)
            return

        def residue(r, carry):
            if nqb == grp:
                unit_group(stage, [block(r, t) for t in range(nqb)])
                return carry
            unit_group(stage, [block(r, t) for t in range(grp)])

            def interior(j, c):
                unit_group(stage, [block(r, grp * j + t) for t in range(grp)])
                return c

            if nqb > 2 * grp:
                lax.fori_loop(1, nqb // grp - 1, interior, 0)
            unit_group(stage, [block(r, nqb - grp + t) for t in range(grp)])
            return carry

        if d == 1:
            residue(0, 0)
        else:
            lax.fori_loop(0, d, residue, 0)

    run_pattern(0, 16, q16, k16, k16p, k16n, v16, v16p, v16n)
    run_pattern(1, 4, q4, k4, k4p, k4n, v4, v4p, v4n)
    run_pattern(2, 1, q1, k1, k1p, k1n, v1, v1p, v1n)


def _attn_call(qkv):
    q1, q4, q16, k1, k4, k16, v1, v4, v16 = qkv
    bsz, _, length, _ = q1.shape
    nt = length // ATT_TILE
    args, specs = [], []
    for d, q, k, v in ((1, q1, k1, v1), (4, q4, k4, v4), (16, q16, k16, v16)):
        ld = ATT_TILE // d
        hb = ld // ATT_RADIUS
        nhb = length // d // ATT_RADIUS
        if d == 1:
            main = pl.BlockSpec((1, ATT_SLABS, ld, LANES), lambda b, i: (b, 0, i, 0))
            prev = pl.BlockSpec((1, ATT_SLABS, ATT_RADIUS, LANES),
                                lambda b, i, hb=hb: (b, 0, jnp.maximum(i * hb - 1, 0), 0))
            nxt = pl.BlockSpec((1, ATT_SLABS, ATT_RADIUS, LANES),
                               lambda b, i, hb=hb, nhb=nhb: (b, 0, jnp.minimum((i + 1) * hb, nhb - 1), 0))
        else:
            main = pl.BlockSpec((1, d, ATT_SLABS, ld, LANES), lambda b, i: (b, 0, 0, i, 0))
            prev = pl.BlockSpec((1, d, ATT_SLABS, ATT_RADIUS, LANES),
                                lambda b, i, hb=hb: (b, 0, 0, jnp.maximum(i * hb - 1, 0), 0))
            nxt = pl.BlockSpec((1, d, ATT_SLABS, ATT_RADIUS, LANES),
                               lambda b, i, hb=hb, nhb=nhb: (b, 0, 0, jnp.minimum((i + 1) * hb, nhb - 1), 0))
        args += [q, k, k, k, v, v, v]
        specs += [main, main, prev, nxt, main, prev, nxt]
    return pl.pallas_call(
        _attn_kernel,
        grid=(bsz, nt),
        in_specs=specs,
        out_specs=pl.BlockSpec((1, ATT_SLABS, ATT_TILE, LANES), lambda b, i: (b, 0, i, 0)),
        out_shape=jax.ShapeDtypeStruct((bsz, ATT_SLABS, length, LANES), BF16),
        scratch_shapes=[pltpu.VMEM((ATT_HEADS, ATT_TILE, LANES), F32),
                        pltpu.VMEM((ATT_HEADS, ATT_TILE, LANES), F32),
                        pltpu.VMEM((4, ATT_QB, ATT_KW), F32)],
        compiler_params=pltpu.CompilerParams(
            dimension_semantics=("parallel", "arbitrary"), vmem_limit_bytes=VMEM_LIMIT),
        name="dilated_attention",
    )(*args)


DN_TILE = 512
DN_ROWS = 4
DN_GROUP = 4
DN_PACK = 2


def _dot_nt(a, b, **kw):
    return lax.dot_general(a, b, (((1,), (1,)), ((), ())), preferred_element_type=F32, **kw)


def _deltanet_kernel(*refs, reverse, tile, rows_per_step):
    if reverse:
        dq_ref, dk_ref, dv_ref, col_ref, row_ref, o_ref, state = refs
    else:
        dq_ref, dk_ref, dv_ref, col_ref, row_ref, gate_ref, obwd_ref, gain_ref, o_ref, state = refs
    c_sz = DN_CHUNK
    nchunks = tile // c_sz

    @pl.when(pl.program_id(1) == 0)
    def _():
        state[...] = jnp.zeros_like(state)

    pack = DN_PACK
    width = pack * c_sz
    ii = lax.broadcasted_iota(jnp.int32, (c_sz, width), 0)
    lane_w = lax.broadcasted_iota(jnp.int32, (1, width), 1)
    jj = lax.broadcasted_iota(jnp.int32, (c_sz, width), 1) % c_sz
    incl = (jj >= ii) if reverse else (jj <= ii)
    strict = (jj > ii) if reverse else (jj < ii)
    edge = 0 if reverse else c_sz - 1
    blk = [(lane_w >= p * c_sz) & (lane_w < (p + 1) * c_sz) for p in range(pack)]

    order = list(range(nchunks - 1, -1, -1)) if reverse else list(range(nchunks))
    rows_of = lambda c: slice(c * c_sz, (c + 1) * c_sz)
    bf = lambda t: t.astype(BF16)
    dot = functools.partial(jnp.dot, preferred_element_type=F32)
    cat = jnp.concatenate

    def blockdiag(t16):
        zero = jnp.zeros_like(t16)
        return cat([jnp.where(blk[p], t16, zero) for p in range(pack)], axis=0)

    pre, pre_qk = {}, {}
    for g0 in range(0, nchunks, DN_GROUP):
        units = [(c, b, hg) for c in order[g0:g0 + DN_GROUP] for b in range(rows_per_step)
                 for hg in range(DN_HEADS // pack)]
        colt = {(c, b): col_ref[b, rows_of(c), :] for c, b, _ in units}
        rowt = {(c, b): row_ref[b, :, rows_of(c)] for c, b, _ in units}
        v = {}
        for un in units:
            c, b, hg = un
            per_head = []
            for h in range(hg * pack, (hg + 1) * pack):
                idx = (DN_HEADS if reverse else 0) + h
                ct = colt[c, b]
                beta = ct[:, idx:idx + 1]
                gcc = ct[:, 2 * DN_HEADS + idx:2 * DN_HEADS + idx + 1]
                gcr = rowt[c, b][2 * DN_HEADS + idx:2 * DN_HEADS + idx + 1, :]
                gl = gcc[edge:edge + 1, :]
                k16 = dk_ref[b, h, rows_of(c), :]
                q16 = dq_ref[b, h, rows_of(c), :]
                kf = k16.astype(F32)
                kbeta = kf * beta
                egc = jnp.exp(gcc)
                vf = dv_ref[b, h, rows_of(c), :].astype(F32)
                per_head.append(dict(
                    gcc=gcc, gcr=gcr, k16=k16, q16=q16, kb16=bf(kbeta), eg=jnp.exp(gl),
                    rhs=bf(cat([vf * beta, kbeta * egc], axis=-1)), rhs32=(vf * beta, kbeta * egc),
                    qd=bf(q16.astype(F32) * egc), kd=bf(kf * jnp.exp(gl - gcc))))
            zero = jnp.zeros_like(per_head[0]["k16"])
            k_diag = cat([cat([hd["k16"] if q == p else zero for q in range(pack)], axis=1)
                          for p, hd in enumerate(per_head)], axis=0)
            gcc_w = per_head[-1]["gcc"]
            for p in range(pack - 2, -1, -1):
                gcc_w = jnp.where(blk[p], per_head[p]["gcc"], gcc_w)
            gdiff = gcc_w - cat([hd["gcr"] for hd in per_head], axis=1)
            decay = jnp.exp(jnp.where(incl, gdiff, NEG_INF))
            v[un] = dict(heads=per_head, decay=decay,
                         kk=_dot_nt(cat([hd["kb16"] for hd in per_head], axis=1), k_diag),
                         qk=_dot_nt(cat([hd["q16"] for hd in per_head], axis=1), k_diag))
        mpow = {un: -jnp.where(strict, v[un]["kk"] * v[un]["decay"], 0.0) for un in units}
        tlow = dict(mpow)
        for _ in range(5):
            m16 = {un: bf(mpow[un]) for un in units}
            mpow = {un: dot(m16[un], blockdiag(m16[un])) for un in units}
            tlow = {un: tlow[un] + mpow[un] + dot(bf(tlow[un]), blockdiag(bf(mpow[un]))) for un in units}
        for un in units:
            c, b, hg = un
            heads = v[un]["heads"]
            sol = dot(blockdiag(bf(tlow[un])), cat([hd["rhs"] for hd in heads], axis=0))
            pre_qk[un] = blockdiag(bf(v[un]["qk"] * v[un]["decay"]))
            for p, hd in enumerate(heads):
                sol_p = sol[p * c_sz:(p + 1) * c_sz]
                pre[c, b, hg * pack + p] = dict(
                    u=hd["rhs32"][0] + sol_p[:, :DN_HEAD_DIM],
                    wq=cat([bf(hd["rhs32"][1] + sol_p[:, DN_HEAD_DIM:]), hd["qd"]], axis=0),
                    kd=hd["kd"], eg=hd["eg"])

    chains = [(b, h) for b in range(rows_per_step) for h in range(DN_HEADS)]
    groups = [(b, hg) for b in range(rows_per_step) for hg in range(DN_HEADS // pack)]
    for c in order:
        st = {n: state[n[0] * DN_HEADS + n[1]] for n in chains}
        st16 = {n: bf(st[n]) for n in chains}
        ws = {n: dot(pre[(c,) + n]["wq"], st16[n]) for n in chains}
        v16 = {n: bf(pre[(c,) + n]["u"] - ws[n][:c_sz]) for n in chains}
        ovg = {g: dot(pre_qk[(c,) + g], cat([v16[g[0], g[1] * pack + p] for p in range(pack)], axis=0))
               for g in groups}
        sv = {n: lax.dot_general(pre[(c,) + n]["kd"], v16[n], (((0,), (0,)), ((), ())),
                                 preferred_element_type=F32) for n in chains}
        for n in chains:
            b, h = n
            state[b * DN_HEADS + h] = st[n] * pre[(c,) + n]["eg"] + sv[n]
            o = ws[n][c_sz:] + ovg[b, h // pack][(h % pack) * c_sz:(h % pack + 1) * c_sz]
            if reverse:
                o_ref[b, h, rows_of(c), :] = o
            else:
                tot = o + obwd_ref[b, h, rows_of(c), :]
                y = tot * _rms_scale(tot) * gain_ref[...] * gate_ref[b, h, rows_of(c), :].astype(F32)
                o_ref[b, h, rows_of(c), :] = y.astype(BF16)


def _deltanet_call(dq, dk, dv, col, row, gate, dn_out_norm, *, tile=DN_TILE):
    bsz, _, length, _ = dq.shape
    rows_per_step = DN_ROWS if bsz % DN_ROWS == 0 else 2
    assert bsz % rows_per_step == 0 and length % tile == 0
    nt = length // tile
    nb = bsz // rows_per_step

    def run(reverse, extra_args, extra_specs, out_dtype):
        pos = (lambda i: nt - 1 - i) if reverse else (lambda i: i)
        slab = pl.BlockSpec((rows_per_step, DN_HEADS, tile, LANES), lambda b, i: (b, 0, pos(i), 0))
        return pl.pallas_call(
            functools.partial(_deltanet_kernel, reverse=reverse, tile=tile, rows_per_step=rows_per_step),
            grid=(nb, nt),
            in_specs=[slab, slab, slab,
                      pl.BlockSpec((rows_per_step, tile, LANES), lambda b, i: (b, pos(i), 0)),
                      pl.BlockSpec((rows_per_step, 16, tile), lambda b, i: (b, 0, pos(i)))] + extra_specs(slab),
            out_specs=slab,
            out_shape=jax.ShapeDtypeStruct((bsz, DN_HEADS, length, LANES), out_dtype),
            scratch_shapes=[pltpu.VMEM((rows_per_step * DN_HEADS, DN_HEAD_DIM, DN_HEAD_DIM), F32)],
            compiler_params=pltpu.CompilerParams(
                dimension_semantics=("parallel", "arbitrary"), vmem_limit_bytes=VMEM_LIMIT),
            name="deltanet_bwd" if reverse else "deltanet_fwd",
        )(dq, dk, dv, col, row, *extra_args)

    o_bwd = run(True, (), lambda slab: [], F32)
    return run(False, (gate, o_bwd, dn_out_norm.reshape(1, DN_HEAD_DIM)),
               lambda slab: [slab, slab, pl.BlockSpec((1, DN_HEAD_DIM), lambda b, i: (0, 0))], BF16)


def _ffn_kernel(x_ref, xp_ref, xn_ref, att_ref, attp_ref, attn_ref, dn_ref, dnp_ref, dnn_ref,
                wout_ref, norm2_ref, wg_ref, wu_ref, cg_ref, cu_ref, wd_ref, o_ref, perm_scr, *, tm):
    i = pl.program_id(1)
    last = pl.num_programs(1) - 1
    gain = norm2_ref[...]
    run = tm // SUBLANES

    half = N_SLABS // 2
    xh = jnp.concatenate([xp_ref[0, HALO_ROWS - 8:, :], xn_ref[0, :8, :]], axis=0)
    mh = ([jnp.concatenate([attp_ref[0, s], attn_ref[0, s]], axis=0) for s in range(half)]
          + [jnp.concatenate([dnp_ref[0, s], dnn_ref[0, s]], axis=0) for s in range(half)])
    mh = [jnp.concatenate([m[HALO_ROWS - 8:HALO_ROWS], m[HALO_ROWS:HALO_ROWS + 8]], axis=0) for m in mh]
    main = [att_ref[0, s] for s in range(half)] + [dn_ref[0, s] for s in range(half)]
    mix = jnp.concatenate([jnp.concatenate([a, b], axis=0) for a, b in zip(main, mh)], axis=-1)
    h_all = (jnp.concatenate([x_ref[0], xh], axis=0)
             + jnp.dot(mix, wout_ref[...], preferred_element_type=F32))
    n_all = h_all * _rms_scale(h_all) * gain
    h = h_all[:tm]

    for s in range(SUBLANES):
        for sl in range(N_SLABS):
            perm_scr[sl, pl.ds(s, run, stride=SUBLANES), :] = n_all[s * run:(s + 1) * run,
                                                                    sl * LANES:(sl + 1) * LANES]
    n2_all = jnp.concatenate([jnp.concatenate([perm_scr[sl] for sl in range(N_SLABS)], axis=-1),
                              n_all[tm:]], axis=0).astype(BF16)
    sub = lax.broadcasted_iota(jnp.int32, (SUBLANES, 1), 0)
    keep_prev = (i > 0).astype(F32)
    keep_next = (i < last).astype(F32)

    def up_proj(c):
        return [jnp.dot(n2_all, w_ref[c], preferred_element_type=F32) for w_ref in (wg_ref, wu_ref)]

    def conv(c, y_all, cw_ref):
        y = y_all[:tm]
        before = y_all[tm + 7:tm + 8] * keep_prev
        after = y_all[tm + 8:tm + 9] * keep_next
        y_prev0 = jnp.where(sub == 0, before, pltpu.roll(y[tm - SUBLANES:], 1, 0))
        y_next_last = jnp.where(sub == SUBLANES - 1, after, pltpu.roll(y[:SUBLANES], SUBLANES - 1, 0))
        y_prev = jnp.concatenate([y_prev0, y[:tm - SUBLANES]], axis=0)
        y_next = jnp.concatenate([y[SUBLANES:], y_next_last], axis=0)
        cw = cw_ref[c]
        return y_prev * cw[0:1] + y * cw[1:2] + y_next * cw[2:3] + cw[3:4]

    acc = jnp.zeros((tm, D_MODEL), F32)
    ahead = up_proj(0)
    for c in range(N_FFN_CHUNKS):
        yg, yu = ahead
        if c + 1 < N_FFN_CHUNKS:
            ahead = up_proj(c + 1)
        act = (_silu(conv(c, yg, cg_ref)) * conv(c, yu, cu_ref)).astype(BF16)
        acc = acc + jnp.dot(act, wd_ref[c], preferred_element_type=F32)
    for sl in range(N_SLABS):
        perm_scr[sl] = acc[:, sl * LANES:(sl + 1) * LANES]
    for s in range(SUBLANES):
        for sl in range(N_SLABS):
            o_ref[0, s * run:(s + 1) * run, sl * LANES:(sl + 1) * LANES] = (
                h[s * run:(s + 1) * run, sl * LANES:(sl + 1) * LANES]
                + perm_scr[sl, pl.ds(s, run, stride=SUBLANES), :])


def _ffn_call(x, att, dn, w_out, norm2, w_up, ffn_conv_w, ffn_conv_b, w_down, *, tm=256):
    bsz, length, _ = x.shape
    nt = length // tm
    hb = tm // HALO_ROWS
    nhb = length // HALO_ROWS

    def chunked_cols(w):
        return w.reshape(w.shape[0], N_FFN_CHUNKS, FFN_CHUNK).transpose(1, 0, 2)

    wg = chunked_cols(w_up[:, :FFN_DIM]).astype(BF16)
    wu = chunked_cols(w_up[:, FFN_DIM:]).astype(BF16)
    wd = w_down.reshape(N_FFN_CHUNKS, FFN_CHUNK, D_MODEL).astype(BF16)

    def conv_table(lo):
        t = jnp.concatenate([ffn_conv_w[:, lo:lo + FFN_DIM], ffn_conv_b[None, lo:lo + FFN_DIM],
                             jnp.zeros((4, FFN_DIM), F32)], axis=0)
        return chunked_cols(t)

    cg, cu = conv_table(0), conv_table(FFN_DIM)

    prev = lambda b, i: (b, jnp.maximum(i * hb - 1, 0), 0)
    nxt = lambda b, i: (b, jnp.minimum((i + 1) * hb, nhb - 1), 0)
    prev4 = lambda b, i: (b, 0, jnp.maximum(i * hb - 1, 0), 0)
    nxt4 = lambda b, i: (b, 0, jnp.minimum((i + 1) * hb, nhb - 1), 0)

    return pl.pallas_call(
        functools.partial(_ffn_kernel, tm=tm),
        grid=(bsz, nt),
        in_specs=[
            pl.BlockSpec((1, tm, D_MODEL), lambda b, i: (b, i, 0)),
            pl.BlockSpec((1, HALO_ROWS, D_MODEL), prev),
            pl.BlockSpec((1, HALO_ROWS, D_MODEL), nxt),
            pl.BlockSpec((1, N_SLABS // 2, tm, LANES), lambda b, i: (b, 0, i, 0)),
            pl.BlockSpec((1, N_SLABS // 2, HALO_ROWS, LANES), prev4),
            pl.BlockSpec((1, N_SLABS // 2, HALO_ROWS, LANES), nxt4),
            pl.BlockSpec((1, N_SLABS // 2, tm, LANES), lambda b, i: (b, 0, i, 0)),
            pl.BlockSpec((1, N_SLABS // 2, HALO_ROWS, LANES), prev4),
            pl.BlockSpec((1, N_SLABS // 2, HALO_ROWS, LANES), nxt4),
            _const_spec((D_MODEL, D_MODEL)),
            _const_spec((1, D_MODEL)),
            _const_spec((N_FFN_CHUNKS, D_MODEL, FFN_CHUNK)),
            _const_spec((N_FFN_CHUNKS, D_MODEL, FFN_CHUNK)),
            _const_spec((N_FFN_CHUNKS, 8, FFN_CHUNK)),
            _const_spec((N_FFN_CHUNKS, 8, FFN_CHUNK)),
            _const_spec((N_FFN_CHUNKS, FFN_CHUNK, D_MODEL)),
        ],
        out_specs=pl.BlockSpec((1, tm, D_MODEL), lambda b, i: (b, i, 0)),
        out_shape=jax.ShapeDtypeStruct((bsz, length, D_MODEL), F32),
        scratch_shapes=[pltpu.VMEM((N_SLABS, tm, LANES), F32)],
        compiler_params=pltpu.CompilerParams(
            dimension_semantics=("parallel", "arbitrary"), vmem_limit_bytes=VMEM_LIMIT),
        name="outproj_convglu",
    )(x, x, x, att, att, att, dn, dn, dn, w_out.astype(BF16), norm2.reshape(1, D_MODEL),
      wg, wu, cg, cu, wd)


def _layer(x, norm1, w_in, att_q_norm, att_k_norm, dn_conv_w, dn_a_log, dn_dt_bias, dn_out_norm,
           w_out, norm2, w_up, ffn_conv_w, ffn_conv_b, w_down):
    outs = _inproj_call(x, norm1, w_in, att_q_norm, att_k_norm, dn_conv_w, dn_a_log, dn_dt_bias)
    att = _attn_call(outs[:9])
    dq, dk, dv, gate, col, row = outs[9:]
    dn = _deltanet_call(dq, dk, dv, col, row, gate, dn_out_norm)
    return _ffn_call(x, att, dn, w_out, norm2, w_up, ffn_conv_w, ffn_conv_b, w_down)


def kernel(x_prompt, x_sample, norm1, w_in, att_q_norm, att_k_norm, dn_conv_w, dn_a_log, dn_dt_bias,
           dn_out_norm, w_out, norm2, w_up, ffn_conv_w, ffn_conv_b, w_down):
    def trunk(x):
        for l in range(norm1.shape[0]):
            x = _layer(x, norm1[l], w_in[l], att_q_norm[l], att_k_norm[l], dn_conv_w[l], dn_a_log[l],
                       dn_dt_bias[l], dn_out_norm[l], w_out[l], norm2[l], w_up[l], ffn_conv_w[l],
                       ffn_conv_b[l], w_down[l])
        return x

    return trunk(x_prompt), trunk(x_sample)
```

```python
import functools
import math

import jax
import jax.numpy as jnp
from jax import lax
from jax.experimental import pallas as pl
from jax.experimental.pallas import tpu as pltpu

D_MODEL = 1024
ATT_HEADS = 8
ATT_HEAD_DIM = 64
ATT_WIDTH = ATT_HEADS * ATT_HEAD_DIM
DN_HEADS = 4
DN_HEAD_DIM = 128
DN_WIDTH = DN_HEADS * DN_HEAD_DIM
DN_CONV = 5
DN_CHUNK = 64
FFN_DIM = 2816
FFN_CONV = 3
NORM_EPS = 1e-6

LANES = 128
SUBLANES = 8
HALO_ROWS = 16
N_SLABS = (ATT_WIDTH + DN_WIDTH) // LANES
FFN_CHUNK = 256
N_FFN_CHUNKS = FFN_DIM // FFN_CHUNK
VMEM_LIMIT = 56 * 1024 * 1024

F32 = jnp.float32
BF16 = jnp.bfloat16


def _rms_scale(xf):
    return lax.rsqrt(jnp.mean(xf * xf, axis=-1, keepdims=True) + NORM_EPS)


def _silu(x):
    return x * (1.0 / (1.0 + jnp.exp(-x)))


def _const_spec(shape):
    nd = len(shape)
    return pl.BlockSpec(shape, lambda b, i: (0,) * nd, pipeline_mode=pl.Buffered(1))


ATT_SLABS = ATT_WIDTH // LANES
DILATIONS = (1, 4, 16)
ROPE_HALF = ATT_HEAD_DIM // 2
ATT_Q_SCALE = math.log2(math.e) / math.sqrt(ATT_HEAD_DIM)


def _inproj_kernel(x_ref, xp_ref, xn_ref, wa_ref, wd_ref, wg_ref, wba_ref, qkg_ref,
                   cos_ref, sin_ref, convw_ref, gpar_ref,
                   q1_ref, q4_ref, q16_ref, k1_ref, k4_ref, k16_ref, v1_ref, v4_ref, v16_ref,
                   dq_ref, dk_ref, dv_ref, gate_ref, col_ref, row_ref,
                   row_scr, perm_scr, mod4_scr, *, tm):
    i = pl.program_id(1)
    last = pl.num_programs(1) - 1
    def normed(x):
        return x * _rms_scale(x)

    nf = normed(x_ref[0])
    n = nf.astype(BF16)
    nh = normed(jnp.concatenate([xp_ref[0, HALO_ROWS - 8:, :], xn_ref[0, :8, :]], axis=0)).astype(BF16)
    run = tm // SUBLANES
    for s in range(SUBLANES):
        for sl in range(D_MODEL // LANES):
            row_scr[sl, pl.ds(s, run, stride=SUBLANES), :] = nf[s * run:(s + 1) * run,
                                                               sl * LANES:(sl + 1) * LANES]
    n_rows = jnp.concatenate([row_scr[sl] for sl in range(D_MODEL // LANES)], axis=-1).astype(BF16)

    att = jnp.dot(n, wa_ref[...], preferred_element_type=F32)
    dn_all = jnp.dot(jnp.concatenate([n_rows, nh], axis=0), wd_ref[...], preferred_element_type=F32)
    g = jnp.dot(n, wg_ref[...], preferred_element_type=F32)
    ba = jnp.dot(n, wba_ref[...], preferred_element_type=F32)

    lane = lax.broadcasted_iota(jnp.int32, (1, LANES), 1)
    gp = gpar_ref[...]
    beta = 1.0 / (1.0 + jnp.exp(-ba))
    z = ba + gp[0:1]
    softplus = jnp.maximum(z, 0.0) + jnp.log(1.0 + jnp.exp(-jnp.abs(z)))
    colv = jnp.where(lane < 2 * DN_HEADS, beta, -jnp.exp(gp[1:2]) * softplus)
    t = colv.T
    gl = t[8:16]
    pos = lax.broadcasted_iota(jnp.int32, (1, tm), 1) % DN_CHUNK
    pre, suf = gl, gl
    step = 1
    while step < DN_CHUNK:
        pre = pre + sum(jnp.where(pos >= m * step, pltpu.roll(pre, m * step, 1), 0.0) for m in (1, 2, 3))
        suf = suf + sum(jnp.where(pos < DN_CHUNK - m * step, pltpu.roll(suf, tm - m * step, 1), 0.0)
                        for m in (1, 2, 3))
        step *= 4
    sub = lax.broadcasted_iota(jnp.int32, (SUBLANES, 1), 0)
    gc = jnp.where(sub < DN_HEADS, pre, suf)
    row_ref[0] = jnp.concatenate([t[0:8], gc], axis=0)
    col_ref[0] = jnp.concatenate([t[0:8], gc, t[16:]], axis=0).T

    head_a = lane < ATT_HEAD_DIM
    first_half = (lane % ATT_HEAD_DIM) < ROPE_HALF
    cos = cos_ref[...]
    sin = sin_ref[...]
    for s in range(ATT_SLABS):
        for which in range(2):
            t = att[:, which * ATT_WIDTH + s * LANES: which * ATT_WIDTH + (s + 1) * LANES]
            ss = t * t
            sa = jnp.sum(jnp.where(head_a, ss, 0.0), axis=-1, keepdims=True)
            sb = jnp.sum(jnp.where(head_a, 0.0, ss), axis=-1, keepdims=True)
            ms = jnp.where(head_a, sa, sb) * (1.0 / ATT_HEAD_DIM)
            tn = t * lax.rsqrt(ms + NORM_EPS) * qkg_ref[which:which + 1, :]
            rot = jnp.where(first_half, pltpu.roll(tn, LANES - ROPE_HALF, 1), pltpu.roll(tn, ROPE_HALF, 1))
            r = tn * cos + rot * sin
            if which == 0:
                r = r * ATT_Q_SCALE
            perm_scr[which * ATT_SLABS + s] = r
        perm_scr[2 * ATT_SLABS + s] = att[:, 2 * ATT_WIDTH + s * LANES: 2 * ATT_WIDTH + (s + 1) * LANES]
    outs = ((q1_ref, q4_ref, q16_ref), (k1_ref, k4_ref, k16_ref), (v1_ref, v4_ref, v16_ref))
    for which in range(3):
        o1, o4, o16 = outs[which]
        for s in range(ATT_SLABS):
            slab = which * ATT_SLABS + s
            o1[0, s] = perm_scr[slab].astype(BF16)
            for r in range(4):
                quarter = perm_scr[slab, pl.ds(r, tm // 4, stride=4), :]
                o4[0, r, s] = quarter.astype(BF16)
                mod4_scr[r] = quarter
            for r in range(4):
                for r2 in range(4):
                    o16[0, r + 4 * r2, s] = mod4_scr[r, pl.ds(r2, tm // 16, stride=4), :].astype(BF16)

    keep_prev = (i > 0).astype(F32)
    keep_next = (i < last).astype(F32)
    sub = lax.broadcasted_iota(jnp.int32, (SUBLANES, 1), 0)
    dn = dn_all[:tm]

    def edge_before(vreg_rows, token_row):
        return jnp.where(sub == 0, dn_all[token_row:token_row + 1] * keep_prev,
                         pltpu.roll(dn[vreg_rows], 1, 0))

    def edge_after(vreg_rows, token_row):
        return jnp.where(sub == SUBLANES - 1, dn_all[token_row:token_row + 1] * keep_next,
                         pltpu.roll(dn[vreg_rows], SUBLANES - 1, 0))

    b1 = edge_before(slice(tm - 8, tm), tm + 7)
    b2 = edge_before(slice(tm - 16, tm - 8), tm + 6)
    a1 = edge_after(slice(0, 8), tm + 8)
    a2 = edge_after(slice(8, 16), tm + 9)
    cat = jnp.concatenate
    cw = convw_ref[...]
    y = (cat([b2, b1, dn[:tm - 16]], axis=0) * cw[0:1] + cat([b1, dn[:tm - 8]], axis=0) * cw[1:2]
         + dn * cw[2:3]
         + cat([dn[8:], a1], axis=0) * cw[3:4] + cat([dn[16:], a1, a2], axis=0) * cw[4:5])
    y = _silu(y)
    for h in range(DN_HEADS):
        qh = y[:, h * LANES:(h + 1) * LANES]
        kh = y[:, DN_WIDTH + h * LANES: DN_WIDTH + (h + 1) * LANES]
        vh = y[:, 2 * DN_WIDTH + h * LANES: 2 * DN_WIDTH + (h + 1) * LANES]
        qh = qh * lax.rsqrt(jnp.sum(qh * qh, axis=-1, keepdims=True) + NORM_EPS) * (DN_HEAD_DIM ** -0.5)
        kh = kh * lax.rsqrt(jnp.sum(kh * kh, axis=-1, keepdims=True) + NORM_EPS)
        for j, (val, ref) in enumerate(((qh, dq_ref), (kh, dk_ref), (vh, dv_ref))):
            row_scr[j] = val
            for s in range(SUBLANES):
                ref[0, h, s * run:(s + 1) * run, :] = row_scr[j, pl.ds(s, run, stride=SUBLANES), :].astype(BF16)
    for h in range(DN_HEADS):
        gate_ref[0, h] = _silu(g[:, h * LANES:(h + 1) * LANES]).astype(BF16)


def _rope_tables(length):
    inv_freq = 1.0 / (10000.0 ** (jnp.arange(ROPE_HALF, dtype=F32) * 2.0 / ATT_HEAD_DIM))
    ang = jnp.arange(length, dtype=F32)[:, None] * inv_freq[None, :]
    cos, sin = jnp.cos(ang), jnp.sin(ang)
    cos128 = jnp.tile(cos, (1, LANES // ROPE_HALF))
    sin128 = jnp.tile(jnp.concatenate([-sin, sin], axis=-1), (1, LANES // ATT_HEAD_DIM))
    return cos128, sin128


def _inproj_call(x, norm1, w_in, att_q_norm, att_k_norm, dn_conv_w, dn_a_log, dn_dt_bias, *, tm=512):
    bsz, length, _ = x.shape
    nt = length // tm
    hb = tm // HALO_ROWS
    nhb = length // HALO_ROWS
    o3 = 3 * ATT_WIDTH
    o4 = o3 + 3 * DN_WIDTH
    o5 = o4 + DN_WIDTH
    w_in = w_in * norm1[:, None]
    wa = w_in[:, :o3].astype(BF16)
    wd = w_in[:, o3:o4].astype(BF16)
    wg = w_in[:, o4:o5].astype(BF16)
    wba = jnp.pad(w_in[:, o5:], ((0, 0), (0, LANES - 4 * DN_HEADS))).astype(BF16)
    qkg = jnp.stack([jnp.tile(att_q_norm, 2), jnp.tile(att_k_norm, 2)]
                    + [jnp.zeros((LANES,), F32)] * 6)
    convw = jnp.concatenate([dn_conv_w, jnp.zeros((3, 3 * DN_WIDTH), F32)], axis=0)
    pad_lanes = lambda v: jnp.pad(v.reshape(-1), (2 * DN_HEADS, LANES - 4 * DN_HEADS))
    gpar = jnp.stack([pad_lanes(dn_dt_bias), pad_lanes(dn_a_log)] + [jnp.zeros((LANES,), F32)] * 6)
    cos128, sin128 = _rope_tables(length)

    prev = lambda b, i: (b, jnp.maximum(i * hb - 1, 0), 0)
    nxt = lambda b, i: (b, jnp.minimum((i + 1) * hb, nhb - 1), 0)
    slab = jax.ShapeDtypeStruct((bsz, ATT_SLABS, length, LANES), BF16)
    slab_spec = pl.BlockSpec((1, ATT_SLABS, tm, LANES), lambda b, i: (b, 0, i, 0))
    out_shape, out_specs = [], []
    for _ in range(3):
        for d in DILATIONS:
            if d == 1:
                out_shape.append(slab)
                out_specs.append(slab_spec)
            else:
                out_shape.append(jax.ShapeDtypeStruct((bsz, d, ATT_SLABS, length // d, LANES), BF16))
                out_specs.append(pl.BlockSpec((1, d, ATT_SLABS, tm // d, LANES),
                                              lambda b, i: (b, 0, 0, i, 0)))
    out_shape += [slab] * 4
    out_specs += [slab_spec] * 4
    out_shape += [jax.ShapeDtypeStruct((bsz, length, LANES), F32),
                  jax.ShapeDtypeStruct((bsz, 16, length), F32)]
    out_specs += [pl.BlockSpec((1, tm, LANES), lambda b, i: (b, i, 0)),
                  pl.BlockSpec((1, 16, tm), lambda b, i: (b, 0, i))]

    return pl.pallas_call(
        functools.partial(_inproj_kernel, tm=tm),
        grid=(bsz, nt),
        in_specs=[
            pl.BlockSpec((1, tm, D_MODEL), lambda b, i: (b, i, 0)),
            pl.BlockSpec((1, HALO_ROWS, D_MODEL), prev),
            pl.BlockSpec((1, HALO_ROWS, D_MODEL), nxt),
            _const_spec((D_MODEL, o3)),
            _const_spec((D_MODEL, 3 * DN_WIDTH)),
            _const_spec((D_MODEL, DN_WIDTH)),
            _const_spec((D_MODEL, LANES)),
            _const_spec((8, LANES)),
            pl.BlockSpec((tm, LANES), lambda b, i: (i, 0)),
            pl.BlockSpec((tm, LANES), lambda b, i: (i, 0)),
            _const_spec((8, 3 * DN_WIDTH)),
            _const_spec((8, LANES)),
        ],
        out_specs=out_specs,
        out_shape=out_shape,
        scratch_shapes=[pltpu.VMEM((D_MODEL // LANES, tm, LANES), F32),
                        pltpu.VMEM((3 * ATT_SLABS, tm, LANES), F32),
                        pltpu.VMEM((4, tm // 4, LANES), F32)],
        compiler_params=pltpu.CompilerParams(
            dimension_semantics=("parallel", "arbitrary"), vmem_limit_bytes=VMEM_LIMIT),
        name="inproj",
    )(x, x, x, wa, wd, wg, wba, qkg, cos128, sin128, convw, gpar)


ATT_TILE = 1024
ATT_QB = 64
ATT_RADIUS = 64
ATT_KW = ATT_QB + 2 * ATT_RADIUS
ATT_GROUP = 4
NEG_INF = -1e30


def _attn_kernel(q1, k1, k1p, k1n, v1, v1p, v1n,
                 q4, k4, k4p, k4n, v4, v4p, v4n,
                 q16, k16, k16p, k16n, v16, v16p, v16n,
                 o_ref, acc_o, acc_m, bias_scr):
    i = pl.program_id(1)
    last = pl.num_programs(1) - 1
    lane = lax.broadcasted_iota(jnp.int32, (1, LANES), 1)
    is_a = lane < ATT_HEAD_DIM

    qi = lax.broadcasted_iota(jnp.int32, (ATT_QB, ATT_KW), 0)
    kj = lax.broadcasted_iota(jnp.int32, (ATT_QB, ATT_KW), 1)
    band = jnp.where((kj >= qi) & (kj <= qi + 2 * ATT_RADIUS), 0.0, NEG_INF)
    first_seq = (i == 0).astype(F32)
    last_seq = (i == last).astype(F32)
    before = jnp.where(kj < ATT_RADIUS, NEG_INF, 0.0)
    after = jnp.where(kj >= ATT_QB + ATT_RADIUS, NEG_INF, 0.0)
    bias_scr[0] = band
    bias_scr[1] = band + first_seq * before
    bias_scr[2] = band + last_seq * after
    bias_scr[3] = band + first_seq * before + last_seq * after

    def unit_group(stage, blocks):
        items = [(bi, s) for bi in range(len(blocks)) for s in range(ATT_SLABS)]
        heads = [(bi, s, hh) for bi, s in items for hh in range(2)]
        s2, vext = {}, {}
        for bi, s in items:
            _, q_get, k_get, v_get, _, _ = blocks[bi]
            qt = q_get(s)
            vw = v_get(s)
            zero = jnp.zeros_like(qt)
            qq = jnp.concatenate([jnp.where(is_a, qt, zero), jnp.where(is_a, zero, qt)], axis=0)
            s2[bi, s] = lax.dot_general(qq, k_get(s), (((1,), (1,)), ((), ())),
                                        preferred_element_type=F32)
            one = jnp.ones_like(vw)
            vext[bi, s] = (jnp.where(is_a, vw, one), jnp.where(is_a, one, vw))
        m_prev, m_new, p16 = {}, {}, {}
        for it in heads:
            bi, s, hh = it
            rows = blocks[bi][4]
            sc = s2[bi, s][hh * ATT_QB:(hh + 1) * ATT_QB] + bias_scr[blocks[bi][0]]
            m_cur = jnp.max(sc, axis=-1, keepdims=True)
            if stage == 0:
                m_new[it] = jnp.broadcast_to(m_cur, (ATT_QB, LANES))
            else:
                m_prev[it] = acc_m[2 * s + hh, rows, :]
                m_new[it] = jnp.maximum(m_prev[it], m_cur)
            p16[it] = jnp.concatenate(
                [jnp.exp2(sc[:, :LANES] - m_new[it]),
                 jnp.exp2(sc[:, LANES:] - m_new[it][:, :ATT_KW - LANES])], axis=-1).astype(BF16)
        o = {it: jnp.dot(p16[it], vext[it[0], it[1]][it[2]], preferred_element_type=F32)
             for it in heads}
        for it in heads:
            bi, s, hh = it
            rows = blocks[bi][4]
            if stage > 0:
                o[it] = o[it] + jnp.exp2(m_prev[it] - m_new[it]) * acc_o[2 * s + hh, rows, :]
            if stage < 2:
                acc_m[2 * s + hh, rows, :] = m_new[it]
                acc_o[2 * s + hh, rows, :] = o[it]
        if stage == 2:
            for bi, s in items:
                norm = [o[bi, s, hh] * (1.0 / pltpu.roll(o[bi, s, hh], ATT_HEAD_DIM, 1)) for hh in range(2)]
                o_ref[0, s, blocks[bi][5], :] = jnp.where(is_a, norm[0], norm[1]).astype(BF16)

    def run_pattern(stage, d, q, k, kp, kn, v, vp, vn):
        ld = ATT_TILE // d
        nqb = ld // ATT_QB

        def at(ref, r, s, rows):
            return ref[0, s, rows, :] if d == 1 else ref[0, r, s, rows, :]

        def full(ref, r, s):
            return ref[0, s] if d == 1 else ref[0, r, s]

        def acc_rows(r, qb):
            start = qb * (ATT_QB * d) + r
            return pl.ds(start, ATT_QB) if d == 1 else pl.ds(start, ATT_QB, stride=d)

        def block(r, qb):
            q_rows = pl.ds(qb * ATT_QB, ATT_QB) if isinstance(qb, int) else \
                pl.ds(pl.multiple_of(qb * ATT_QB, ATT_QB), ATT_QB)
            q_get = lambda s: at(q, r, s, q_rows)
            if nqb == 1:
                pieces = lambda m, mp, mn: (lambda s: jnp.concatenate(
                    [full(mp, r, s), full(m, r, s), full(mn, r, s)], axis=0))
                return (3, q_get, pieces(k, kp, kn), pieces(v, vp, vn), acc_rows(r, 0), q_rows)
            if isinstance(qb, int) and qb == 0:
                head_rows = pl.ds(0, ATT_QB + ATT_RADIUS)
                pieces = lambda m, mp: (lambda s: jnp.concatenate(
                    [full(mp, r, s), at(m, r, s, head_rows)], axis=0))
                return (1, q_get, pieces(k, kp), pieces(v, vp), acc_rows(r, 0), q_rows)
            if isinstance(qb, int) and qb == nqb - 1:
                tail_rows = pl.ds(ld - ATT_QB - ATT_RADIUS, ATT_QB + ATT_RADIUS)
                pieces = lambda m, mn: (lambda s: jnp.concatenate(
                    [at(m, r, s, tail_rows), full(mn, r, s)], axis=0))
                return (2, q_get, pieces(k, kn), pieces(v, vn), acc_rows(r, qb), q_rows)
            start = qb * ATT_QB - ATT_RADIUS
            win = pl.ds(start if isinstance(qb, int) else pl.multiple_of(start, ATT_RADIUS), ATT_KW)
            return (0, q_get, lambda s: at(k, r, s, win), lambda s: at(v, r, s, win),
                    acc_rows(r, qb), q_rows)

        grp = ATT_GROUP
        if nqb == 1:
            def residues(j, carry):
                unit_group(stage, [block(grp * j + t, 0) for t in range(grp)])
                return carry
            lax.fori_loop(0, d // grp, residues, 0)
            return

        def residue(r, carry):
            if nqb == grp:
                unit_group(stage, [block(r, t) for t in range(nqb)])
                return carry
            unit_group(stage, [block(r, t) for t in range(grp)])

            def interior(j, c):
                unit_group(stage, [block(r, grp * j + t) for t in range(grp)])
                return c

            if nqb > 2 * grp:
                lax.fori_loop(1, nqb // grp - 1, interior, 0)
            unit_group(stage, [block(r, nqb - grp + t) for t in range(grp)])
            return carry

        if d == 1:
            residue(0, 0)
        else:
            lax.fori_loop(0, d, residue, 0)

    run_pattern(0, 16, q16, k16, k16p, k16n, v16, v16p, v16n)
    run_pattern(1, 4, q4, k4, k4p, k4n, v4, v4p, v4n)
    run_pattern(2, 1, q1, k1, k1p, k1n, v1, v1p, v1n)


def _attn_call(qkv):
    q1, q4, q16, k1, k4, k16, v1, v4, v16 = qkv
    bsz, _, length, _ = q1.shape
    nt = length // ATT_TILE
    args, specs = [], []
    for d, q, k, v in ((1, q1, k1, v1), (4, q4, k4, v4), (16, q16, k16, v16)):
        ld = ATT_TILE // d
        hb = ld // ATT_RADIUS
        nhb = length // d // ATT_RADIUS
        if d == 1:
            main = pl.BlockSpec((1, ATT_SLABS, ld, LANES), lambda b, i: (b, 0, i, 0))
            prev = pl.BlockSpec((1, ATT_SLABS, ATT_RADIUS, LANES),
                                lambda b, i, hb=hb: (b, 0, jnp.maximum(i * hb - 1, 0), 0))
            nxt = pl.BlockSpec((1, ATT_SLABS, ATT_RADIUS, LANES),
                               lambda b, i, hb=hb, nhb=nhb: (b, 0, jnp.minimum((i + 1) * hb, nhb - 1), 0))
        else:
            main = pl.BlockSpec((1, d, ATT_SLABS, ld, LANES), lambda b, i: (b, 0, 0, i, 0))
            prev = pl.BlockSpec((1, d, ATT_SLABS, ATT_RADIUS, LANES),
                                lambda b, i, hb=hb: (b, 0, 0, jnp.maximum(i * hb - 1, 0), 0))
            nxt = pl.BlockSpec((1, d, ATT_SLABS, ATT_RADIUS, LANES),
                               lambda b, i, hb=hb, nhb=nhb: (b, 0, 0, jnp.minimum((i + 1) * hb, nhb - 1), 0))
        args += [q, k, k, k, v, v, v]
        specs += [main, main, prev, nxt, main, prev, nxt]
    return pl.pallas_call(
        _attn_kernel,
        grid=(bsz, nt),
        in_specs=specs,
        out_specs=pl.BlockSpec((1, ATT_SLABS, ATT_TILE, LANES), lambda b, i: (b, 0, i, 0)),
        out_shape=jax.ShapeDtypeStruct((bsz, ATT_SLABS, length, LANES), BF16),
        scratch_shapes=[pltpu.VMEM((ATT_HEADS, ATT_TILE, LANES), F32),
                        pltpu.VMEM((ATT_HEADS, ATT_TILE, LANES), F32),
                        pltpu.VMEM((4, ATT_QB, ATT_KW), F32)],
        compiler_params=pltpu.CompilerParams(
            dimension_semantics=("parallel", "arbitrary"), vmem_limit_bytes=VMEM_LIMIT),
        name="dilated_attention",
    )(*args)


DN_TILE = 512
DN_ROWS = 4
DN_GROUP = 4
DN_PACK = 2


def _dot_nt(a, b, **kw):
    return lax.dot_general(a, b, (((1,), (1,)), ((), ())), preferred_element_type=F32, **kw)


def _deltanet_kernel(*refs, reverse, tile, rows_per_step):
    if reverse:
        dq_ref, dk_ref, dv_ref, col_ref, row_ref, o_ref, state = refs
    else:
        dq_ref, dk_ref, dv_ref, col_ref, row_ref, gate_ref, obwd_ref, gain_ref, o_ref, state = refs
    c_sz = DN_CHUNK
    nchunks = tile // c_sz

    @pl.when(pl.program_id(1) == 0)
    def _():
        state[...] = jnp.zeros_like(state)

    pack = DN_PACK
    width = pack * c_sz
    ii = lax.broadcasted_iota(jnp.int32, (c_sz, width), 0)
    lane_w = lax.broadcasted_iota(jnp.int32, (1, width), 1)
    jj = lax.broadcasted_iota(jnp.int32, (c_sz, width), 1) % c_sz
    incl = (jj >= ii) if reverse else (jj <= ii)
    strict = (jj > ii) if reverse else (jj < ii)
    edge = 0 if reverse else c_sz - 1
    blk = [(lane_w >= p * c_sz) & (lane_w < (p + 1) * c_sz) for p in range(pack)]

    order = list(range(nchunks - 1, -1, -1)) if reverse else list(range(nchunks))
    rows_of = lambda c: slice(c * c_sz, (c + 1) * c_sz)
    bf = lambda t: t.astype(BF16)
    dot = functools.partial(jnp.dot, preferred_element_type=F32)
    cat = jnp.concatenate

    def blockdiag(t16):
        zero = jnp.zeros_like(t16)
        return cat([jnp.where(blk[p], t16, zero) for p in range(pack)], axis=0)

    pre, pre_qk = {}, {}
    for g0 in range(0, nchunks, DN_GROUP):
        units = [(c, b, hg) for c in order[g0:g0 + DN_GROUP] for b in range(rows_per_step)
                 for hg in range(DN_HEADS // pack)]
        colt = {(c, b): col_ref[b, rows_of(c), :] for c, b, _ in units}
        rowt = {(c, b): row_ref[b, :, rows_of(c)] for c, b, _ in units}
        v = {}
        for un in units:
            c, b, hg = un
            per_head = []
            for h in range(hg * pack, (hg + 1) * pack):
                idx = (DN_HEADS if reverse else 0) + h
                ct = colt[c, b]
                beta = ct[:, idx:idx + 1]
                gcc = ct[:, 2 * DN_HEADS + idx:2 * DN_HEADS + idx + 1]
                gcr = rowt[c, b][2 * DN_HEADS + idx:2 * DN_HEADS + idx + 1, :]
                gl = gcc[edge:edge + 1, :]
                k16 = dk_ref[b, h, rows_of(c), :]
                q16 = dq_ref[b, h, rows_of(c), :]
                kf = k16.astype(F32)
                kbeta = kf * beta
                egc = jnp.exp(gcc)
                vf = dv_ref[b, h, rows_of(c), :].astype(F32)
                per_head.append(dict(
                    gcc=gcc, gcr=gcr, k16=k16, q16=q16, kb16=bf(kbeta), eg=jnp.exp(gl),
                    rhs=bf(cat([vf * beta, kbeta * egc], axis=-1)), rhs32=(vf * beta, kbeta * egc),
                    qd=bf(q16.astype(F32) * egc), kd=bf(kf * jnp.exp(gl - gcc))))
            zero = jnp.zeros_like(per_head[0]["k16"])
            k_diag = cat([cat([hd["k16"] if q == p else zero for q in range(pack)], axis=1)
                          for p, hd in enumerate(per_head)], axis=0)
            gcc_w = per_head[-1]["gcc"]
            for p in range(pack - 2, -1, -1):
                gcc_w = jnp.where(blk[p], per_head[p]["gcc"], gcc_w)
            gdiff = gcc_w - cat([hd["gcr"] for hd in per_head], axis=1)
            decay = jnp.exp(jnp.where(incl, gdiff, NEG_INF))
            v[un] = dict(heads=per_head, decay=decay,
                         kk=_dot_nt(cat([hd["kb16"] for hd in per_head], axis=1), k_diag),
                         qk=_dot_nt(cat([hd["q16"] for hd in per_head], axis=1), k_diag))
        mpow = {un: -jnp.where(strict, v[un]["kk"] * v[un]["decay"], 0.0) for un in units}
        tlow = dict(mpow)
        for _ in range(5):
            m16 = {un: bf(mpow[un]) for un in units}
            mpow = {un: dot(m16[un], blockdiag(m16[un])) for un in units}
            tlow = {un: tlow[un] + mpow[un] + dot(bf(tlow[un]), blockdiag(bf(mpow[un]))) for un in units}
        for un in units:
            c, b, hg = un
            heads = v[un]["heads"]
            sol = dot(blockdiag(bf(tlow[un])), cat([hd["rhs"] for hd in heads], axis=0))
            pre_qk[un] = blockdiag(bf(v[un]["qk"] * v[un]["decay"]))
            for p, hd in enumerate(heads):
                sol_p = sol[p * c_sz:(p + 1) * c_sz]
                pre[c, b, hg * pack + p] = dict(
                    u=hd["rhs32"][0] + sol_p[:, :DN_HEAD_DIM],
                    wq=cat([bf(hd["rhs32"][1] + sol_p[:, DN_HEAD_DIM:]), hd["qd"]], axis=0),
                    kd=hd["kd"], eg=hd["eg"])

    chains = [(b, h) for b in range(rows_per_step) for h in range(DN_HEADS)]
    groups = [(b, hg) for b in range(rows_per_step) for hg in range(DN_HEADS // pack)]
    for c in order:
        st = {n: state[n[0] * DN_HEADS + n[1]] for n in chains}
        st16 = {n: bf(st[n]) for n in chains}
        ws = {n: dot(pre[(c,) + n]["wq"], st16[n]) for n in chains}
        v16 = {n: bf(pre[(c,) + n]["u"] - ws[n][:c_sz]) for n in chains}
        ovg = {g: dot(pre_qk[(c,) + g], cat([v16[g[0], g[1] * pack + p] for p in range(pack)], axis=0))
               for g in groups}
        sv = {n: lax.dot_general(pre[(c,) + n]["kd"], v16[n], (((0,), (0,)), ((), ())),
                                 preferred_element_type=F32) for n in chains}
        for n in chains:
            b, h = n
            state[b * DN_HEADS + h] = st[n] * pre[(c,) + n]["eg"] + sv[n]
            o = ws[n][c_sz:] + ovg[b, h // pack][(h % pack) * c_sz:(h % pack + 1) * c_sz]
            if reverse:
                o_ref[b, h, rows_of(c), :] = o
            else:
                tot = o + obwd_ref[b, h, rows_of(c), :]
                y = tot * _rms_scale(tot) * gain_ref[...] * gate_ref[b, h, rows_of(c), :].astype(F32)
                o_ref[b, h, rows_of(c), :] = y.astype(BF16)


def _deltanet_call(dq, dk, dv, col, row, gate, dn_out_norm, *, tile=DN_TILE):
    bsz, _, length, _ = dq.shape
    rows_per_step = DN_ROWS if bsz % DN_ROWS == 0 else 2
    assert bsz % rows_per_step == 0 and length % tile == 0
    nt = length // tile
    nb = bsz // rows_per_step

    def run(reverse, extra_args, extra_specs, out_dtype):
        pos = (lambda i: nt - 1 - i) if reverse else (lambda i: i)
        slab = pl.BlockSpec((rows_per_step, DN_HEADS, tile, LANES), lambda b, i: (b, 0, pos(i), 0))
        return pl.pallas_call(
            functools.partial(_deltanet_kernel, reverse=reverse, tile=tile, rows_per_step=rows_per_step),
            grid=(nb, nt),
            in_specs=[slab, slab, slab,
                      pl.BlockSpec((rows_per_step, tile, LANES), lambda b, i: (b, pos(i), 0)),
                      pl.BlockSpec((rows_per_step, 16, tile), lambda b, i: (b, 0, pos(i)))] + extra_specs(slab),
            out_specs=slab,
            out_shape=jax.ShapeDtypeStruct((bsz, DN_HEADS, length, LANES), out_dtype),
            scratch_shapes=[pltpu.VMEM((rows_per_step * DN_HEADS, DN_HEAD_DIM, DN_HEAD_DIM), F32)],
            compiler_params=pltpu.CompilerParams(
                dimension_semantics=("parallel", "arbitrary"), vmem_limit_bytes=VMEM_LIMIT),
            name="deltanet_bwd" if reverse else "deltanet_fwd",
        )(dq, dk, dv, col, row, *extra_args)

    o_bwd = run(True, (), lambda slab: [], F32)
    return run(False, (gate, o_bwd, dn_out_norm.reshape(1, DN_HEAD_DIM)),
               lambda slab: [slab, slab, pl.BlockSpec((1, DN_HEAD_DIM), lambda b, i: (0, 0))], BF16)


def _ffn_kernel(x_ref, xp_ref, xn_ref, att_ref, attp_ref, attn_ref, dn_ref, dnp_ref, dnn_ref,
                wout_ref, wg_ref, wu_ref, cg_ref, cu_ref, wd_ref, o_ref, perm_scr, *, tm):
    i = pl.program_id(1)
    last = pl.num_programs(1) - 1
    run = tm // SUBLANES

    half = N_SLABS // 2
    xh = jnp.concatenate([xp_ref[0, HALO_ROWS - 8:, :], xn_ref[0, :8, :]], axis=0)
    mh = ([jnp.concatenate([attp_ref[0, s], attn_ref[0, s]], axis=0) for s in range(half)]
          + [jnp.concatenate([dnp_ref[0, s], dnn_ref[0, s]], axis=0) for s in range(half)])
    mh = [jnp.concatenate([m[HALO_ROWS - 8:HALO_ROWS], m[HALO_ROWS:HALO_ROWS + 8]], axis=0) for m in mh]
    main = [att_ref[0, s] for s in range(half)] + [dn_ref[0, s] for s in range(half)]
    mix = jnp.concatenate([jnp.concatenate([a, b], axis=0) for a, b in zip(main, mh)], axis=-1)
    h_all = (jnp.concatenate([x_ref[0], xh], axis=0)
             + jnp.dot(mix, wout_ref[...], preferred_element_type=F32))
    n_all = h_all * _rms_scale(h_all)
    h = h_all[:tm]

    for s in range(SUBLANES):
        for sl in range(N_SLABS):
            perm_scr[sl, pl.ds(s, run, stride=SUBLANES), :] = n_all[s * run:(s + 1) * run,
                                                                    sl * LANES:(sl + 1) * LANES]
    n2_all = jnp.concatenate([jnp.concatenate([perm_scr[sl] for sl in range(N_SLABS)], axis=-1),
                              n_all[tm:]], axis=0).astype(BF16)
    sub = lax.broadcasted_iota(jnp.int32, (SUBLANES, 1), 0)
    keep_prev = (i > 0).astype(F32)
    keep_next = (i < last).astype(F32)

    def up_proj(c):
        return [jnp.dot(n2_all, w_ref[c], preferred_element_type=F32) for w_ref in (wg_ref, wu_ref)]

    def conv(c, y_all, cw_ref):
        y = y_all[:tm]
        before = y_all[tm + 7:tm + 8] * keep_prev
        after = y_all[tm + 8:tm + 9] * keep_next
        y_prev0 = jnp.where(sub == 0, before, pltpu.roll(y[tm - SUBLANES:], 1, 0))
        y_next_last = jnp.where(sub == SUBLANES - 1, after, pltpu.roll(y[:SUBLANES], SUBLANES - 1, 0))
        y_prev = jnp.concatenate([y_prev0, y[:tm - SUBLANES]], axis=0)
        y_next = jnp.concatenate([y[SUBLANES:], y_next_last], axis=0)
        cw = cw_ref[c]
        return y_prev * cw[0:1] + y * cw[1:2] + y_next * cw[2:3] + cw[3:4]

    acc = jnp.zeros((tm, D_MODEL), F32)
    ahead = up_proj(0)
    for c in range(N_FFN_CHUNKS):
        yg, yu = ahead
        if c + 1 < N_FFN_CHUNKS:
            ahead = up_proj(c + 1)
        act = (_silu(conv(c, yg, cg_ref)) * conv(c, yu, cu_ref)).astype(BF16)
        acc = acc + jnp.dot(act, wd_ref[c], preferred_element_type=F32)
    for sl in range(N_SLABS):
        perm_scr[sl] = acc[:, sl * LANES:(sl + 1) * LANES]
    for s in range(SUBLANES):
        for sl in range(N_SLABS):
            o_ref[0, s * run:(s + 1) * run, sl * LANES:(sl + 1) * LANES] = (
                h[s * run:(s + 1) * run, sl * LANES:(sl + 1) * LANES]
                + perm_scr[sl, pl.ds(s, run, stride=SUBLANES), :])


def _ffn_call(x, att, dn, w_out, norm2, w_up, ffn_conv_w, ffn_conv_b, w_down, *, tm=256):
    bsz, length, _ = x.shape
    nt = length // tm
    hb = tm // HALO_ROWS
    nhb = length // HALO_ROWS

    def chunked_cols(w):
        return w.reshape(w.shape[0], N_FFN_CHUNKS, FFN_CHUNK).transpose(1, 0, 2)

    w_up = w_up * norm2[:, None]
    wg = chunked_cols(w_up[:, :FFN_DIM]).astype(BF16)
    wu = chunked_cols(w_up[:, FFN_DIM:]).astype(BF16)
    wd = w_down.reshape(N_FFN_CHUNKS, FFN_CHUNK, D_MODEL).astype(BF16)

    def conv_table(lo):
        t = jnp.concatenate([ffn_conv_w[:, lo:lo + FFN_DIM], ffn_conv_b[None, lo:lo + FFN_DIM],
                             jnp.zeros((4, FFN_DIM), F32)], axis=0)
        return chunked_cols(t)

    cg, cu = conv_table(0), conv_table(FFN_DIM)

    prev = lambda b, i: (b, jnp.maximum(i * hb - 1, 0), 0)
    nxt = lambda b, i: (b, jnp.minimum((i + 1) * hb, nhb - 1), 0)
    prev4 = lambda b, i: (b, 0, jnp.maximum(i * hb - 1, 0), 0)
    nxt4 = lambda b, i: (b, 0, jnp.minimum((i + 1) * hb, nhb - 1), 0)

    return pl.pallas_call(
        functools.partial(_ffn_kernel, tm=tm),
        grid=(bsz, nt),
        in_specs=[
            pl.BlockSpec((1, tm, D_MODEL), lambda b, i: (b, i, 0)),
            pl.BlockSpec((1, HALO_ROWS, D_MODEL), prev),
            pl.BlockSpec((1, HALO_ROWS, D_MODEL), nxt),
            pl.BlockSpec((1, N_SLABS // 2, tm, LANES), lambda b, i: (b, 0, i, 0)),
            pl.BlockSpec((1, N_SLABS // 2, HALO_ROWS, LANES), prev4),
            pl.BlockSpec((1, N_SLABS // 2, HALO_ROWS, LANES), nxt4),
            pl.BlockSpec((1, N_SLABS // 2, tm, LANES), lambda b, i: (b, 0, i, 0)),
            pl.BlockSpec((1, N_SLABS // 2, HALO_ROWS, LANES), prev4),
            pl.BlockSpec((1, N_SLABS // 2, HALO_ROWS, LANES), nxt4),
            _const_spec((D_MODEL, D_MODEL)),
            _const_spec((N_FFN_CHUNKS, D_MODEL, FFN_CHUNK)),
            _const_spec((N_FFN_CHUNKS, D_MODEL, FFN_CHUNK)),
            _const_spec((N_FFN_CHUNKS, 8, FFN_CHUNK)),
            _const_spec((N_FFN_CHUNKS, 8, FFN_CHUNK)),
            _const_spec((N_FFN_CHUNKS, FFN_CHUNK, D_MODEL)),
        ],
        out_specs=pl.BlockSpec((1, tm, D_MODEL), lambda b, i: (b, i, 0)),
        out_shape=jax.ShapeDtypeStruct((bsz, length, D_MODEL), F32),
        scratch_shapes=[pltpu.VMEM((N_SLABS, tm, LANES), F32)],
        compiler_params=pltpu.CompilerParams(
            dimension_semantics=("parallel", "arbitrary"), vmem_limit_bytes=VMEM_LIMIT),
        name="outproj_convglu",
    )(x, x, x, att, att, att, dn, dn, dn, w_out.astype(BF16), wg, wu, cg, cu, wd)


def _layer(x, norm1, w_in, att_q_norm, att_k_norm, dn_conv_w, dn_a_log, dn_dt_bias, dn_out_norm,
           w_out, norm2, w_up, ffn_conv_w, ffn_conv_b, w_down):
    outs = _inproj_call(x, norm1, w_in, att_q_norm, att_k_norm, dn_conv_w, dn_a_log, dn_dt_bias)
    att = _attn_call(outs[:9])
    dq, dk, dv, gate, col, row = outs[9:]
    dn = _deltanet_call(dq, dk, dv, col, row, gate, dn_out_norm)
    return _ffn_call(x, att, dn, w_out, norm2, w_up, ffn_conv_w, ffn_conv_b, w_down)


def kernel(x_prompt, x_sample, norm1, w_in, att_q_norm, att_k_norm, dn_conv_w, dn_a_log, dn_dt_bias,
           dn_out_norm, w_out, norm2, w_up, ffn_conv_w, ffn_conv_b, w_down):
    def trunk(x):
        for l in range(norm1.shape[0]):
            x = _layer(x, norm1[l], w_in[l], att_q_norm[l], att_k_norm[l], dn_conv_w[l], dn_a_log[l],
                       dn_dt_bias[l], dn_out_norm[l], w_out[l], norm2[l], w_up[l], ffn_conv_w[l],
                       ffn_conv_b[l], w_down[l])
        return x

    return trunk(x_prompt), trunk(x_sample)
```
